```python
import jax, jax.numpy as jnp
from jax import lax
import numpy as np

D_MODEL = 1024
BATCH = 4
SEQ = 8192
DEPTH = 2

BRANCH_WIDTH = D_MODEL // 2
RWKV_HEAD_DIM = 64
RWKV_HEADS = BRANCH_WIDTH // RWKV_HEAD_DIM
RWKV_WIDTH = RWKV_HEADS * RWKV_HEAD_DIM
DECAY_LORA = 64
ICLR_LORA = 64
GATE_LORA = 128
RWKV_LN_EPS = 64e-5
CONV_CH = BRANCH_WIDTH
CONV_TAPS = 3
FOX_HEAD_DIM = 64
FOX_HEADS = BRANCH_WIDTH // FOX_HEAD_DIM
FOX_WIDTH = FOX_HEADS * FOX_HEAD_DIM
Q_BLOCK = 128
N_BRANCHES = 3
RWKV_COLS = 3 * RWKV_WIDTH + DECAY_LORA + ICLR_LORA + GATE_LORA
CONV_COLS = 3 * CONV_CH
FOX_COLS = 3 * FOX_WIDTH + FOX_HEADS
GATE_COLS = N_BRANCHES * D_MODEL
IN_COLS = RWKV_COLS + CONV_COLS + FOX_COLS + GATE_COLS
D_FF = 2816
N_EXPERTS = 8
TOP_K = 2
D_FF_EXPERT = 3584
N_DENSE = (DEPTH + 1) // 2
N_MOE = DEPTH // 2
DEEPNORM_ALPHA = (2 * DEPTH) ** 0.25
DEEPNORM_BETA = (8 * DEPTH) ** -0.25
LN_EPS = 1e-5

kernel_name = 'hybrid_rwkv7_conv_fox_moe_deepnorm'


def _split_cols(t, sizes):
    return jnp.split(t, np.cumsum(sizes)[:-1].tolist(), axis=-1)


def _heads(t, n_heads):
    return t.reshape(t.shape[:-1] + (n_heads, t.shape[-1] // n_heads))


def layer_norm(x, g, b):
    xf = x.astype(jnp.float32)
    mu = jnp.mean(xf, axis=-1, keepdims=True)
    var = jnp.mean(jnp.square(xf - mu), axis=-1, keepdims=True)
    return ((xf - mu) * lax.rsqrt(var + LN_EPS) * g + b).astype(x.dtype)


def token_shift(t):
    return jnp.pad(t, ((0, 0), (1, 0), (0, 0)))[:, :-1]


def rwkv7_scan(r, decay, k, v, a_vec, b_vec):
    bsz, _, h, n = r.shape

    def step(state, inp):
        r_t, d_t, k_t, v_t, a_t, b_t = inp
        sa = jnp.einsum('bhvk,bhk->bhv', state, a_t)
        state = (state * d_t[:, :, None, :] + sa[..., None] * b_t[:, :, None, :]
                 + v_t[..., None] * k_t[:, :, None, :])
        return state, jnp.einsum('bhvk,bhk->bhv', state, r_t)

    xs = tuple(jnp.moveaxis(t, 1, 0) for t in (r, decay, k, v, a_vec, b_vec))
    s0 = jnp.zeros((bsz, h, n, n), jnp.float32)
    _, ys = lax.scan(step, s0, xs)
    return jnp.moveaxis(ys, 0, 1)


def rwkv7_time_mix(p, mu_shift, w0_decay, w2_decay, a0, w2_iclr, w2_gate, k_k, k_a, r_k, lnx_g, lnx_b):
    f32 = jnp.float32
    bsz, t_len, _ = p.shape
    p = p + mu_shift * (token_shift(p) - p)
    r, k, v, w_lo, a_lo, g_lo = _split_cols(p, [RWKV_WIDTH] * 3 + [DECAY_LORA, ICLR_LORA, GATE_LORA])
    w_raw = (w0_decay + jnp.tanh(w_lo) @ w2_decay).astype(f32)
    decay = jnp.exp(-jnp.exp(-jax.nn.softplus(-w_raw) - 0.5))
    a = jax.nn.sigmoid(a0 + a_lo @ w2_iclr).astype(f32)
    g = jax.nn.sigmoid(g_lo) @ w2_gate
    r, k, v = r.astype(f32), k.astype(f32), v.astype(f32)
    kk = _heads(k * k_k, RWKV_HEADS)
    kk = kk / jnp.maximum(jnp.linalg.norm(kk, axis=-1, keepdims=True), 1e-12)
    k = k * (1.0 + (a - 1.0) * k_a)
    rh, kh, vh, ah = (_heads(t, RWKV_HEADS) for t in (r, k, v, a))
    y = rwkv7_scan(rh, _heads(decay, RWKV_HEADS), kh, vh, -kk, kk * ah)
    mu = jnp.mean(y, axis=-1, keepdims=True)
    var = jnp.mean(jnp.square(y - mu), axis=-1, keepdims=True)
    y = ((y - mu) * lax.rsqrt(var + RWKV_LN_EPS)).reshape(bsz, t_len, RWKV_WIDTH) * lnx_g + lnx_b
    bonus = jnp.sum(rh * kh * r_k, axis=-1, keepdims=True) * vh
    y = y + bonus.reshape(bsz, t_len, RWKV_WIDTH)
    return y.astype(p.dtype) * g


def short_conv_mix(p, conv_w):
    t_len = p.shape[1]
    b_gate, c_gate, h = _split_cols(p, [CONV_CH] * 3)
    u = c_gate * h
    u_pad = jnp.pad(u, ((0, 0), (CONV_TAPS - 1, 0), (0, 0)))
    y = conv_w[0] * u_pad[:, 0:t_len]
    for i in range(1, CONV_TAPS):
        y = y + conv_w[i] * u_pad[:, i:i + t_len]
    return b_gate * y


def forgetting_attention(p, b_forget):
    f32 = jnp.float32
    bsz, t_len, _ = p.shape
    q, k, v, f = _split_cols(p, [FOX_WIDTH] * 3 + [FOX_HEADS])
    q, k, v = (_heads(t, FOX_HEADS) for t in (q, k, v))
    log_f = jax.nn.log_sigmoid((f + b_forget).astype(f32))
    cum = jnp.transpose(jnp.cumsum(log_f, axis=1), (0, 2, 1))
    scale = FOX_HEAD_DIM ** -0.5
    q_pos = jnp.arange(Q_BLOCK)
    outs = []
    for blk in range(t_len // Q_BLOCK):
        lo = blk * Q_BLOCK
        hi = lo + Q_BLOCK
        s = jnp.einsum('bqhd,bkhd->bhqk', q[:, lo:hi], k[:, :hi]).astype(f32) * scale
        s = s + cum[:, :, lo:hi, None] - cum[:, :, None, :hi]
        causal = jnp.arange(hi)[None, :] <= (lo + q_pos)[:, None]
        s = jnp.where(causal, s, -jnp.inf)
        prob = jax.nn.softmax(s, axis=-1).astype(v.dtype)
        outs.append(jnp.einsum('bhqk,bkhd->bqhd', prob, v[:, :hi]))
    return jnp.concatenate(outs, axis=1).reshape(bsz, t_len, FOX_WIDTH)


def hybrid_token_mixer(x, w_in, mu_shift, w0_decay, w2_decay, a0, w2_iclr, w2_gate, k_k, k_a, r_k,
                       lnx_g, lnx_b, conv_w, b_forget, b_gate, w_up_rwkv, w_up_conv, w_up_attn, w_out):
    bsz, t_len, _ = x.shape
    proj = x @ w_in
    p_rwkv, p_conv, p_fox, p_gate = _split_cols(proj, [RWKV_COLS, CONV_COLS, FOX_COLS, GATE_COLS])
    o_a = rwkv7_time_mix(p_rwkv, mu_shift, w0_decay, w2_decay, a0, w2_iclr, w2_gate, k_k, k_a, r_k, lnx_g, lnx_b)
    o_b = short_conv_mix(p_conv, conv_w)
    o_c = forgetting_attention(p_fox, b_forget)
    gates = jax.nn.sigmoid(p_gate.reshape(bsz, t_len, N_BRANCHES, D_MODEL) + b_gate)
    merged = (gates[:, :, 0] * (o_a @ w_up_rwkv) + gates[:, :, 1] * (o_b @ w_up_conv)
              + gates[:, :, 2] * (o_c @ w_up_attn))
    return merged @ w_out


def swiglu(x, w1, w3, w2):
    return (jax.nn.silu(x @ w1) * (x @ w3)) @ w2


def moe_swiglu(x, router_w, router_b, w1, w3, w2):
    f32 = jnp.float32
    logits = (x @ router_w).astype(f32) + router_b
    top_v, top_i = lax.top_k(logits, TOP_K)
    top_g = jax.nn.softmax(top_v, axis=-1)
    combine = jnp.sum(jax.nn.one_hot(top_i, N_EXPERTS, dtype=f32) * top_g[..., None], axis=-2)
    y = jnp.zeros_like(x)
    for e in range(N_EXPERTS):
        y = y + combine[..., e:e + 1].astype(x.dtype) * swiglu(x, w1[e], w3[e], w2[e])
    return y


def setup_inputs(seed: int = 0) -> dict:
    key = jax.random.key(seed)
    ks = iter(jax.random.split(key, 48))
    f32 = jnp.float32
    L = DEPTH
    D = D_MODEL

    def nrm(shape, scale):
        return jax.random.normal(next(ks), shape, f32) * scale

    return {
        'x': nrm((BATCH, SEQ, D), 1.0),
        'ln0_g': 1.0 + nrm((D,), 0.02),
        'ln0_b': nrm((D,), 0.02),
        'w_in': nrm((L, D, IN_COLS), D ** -0.5),
        'mu_shift': jax.random.uniform(next(ks), (L, RWKV_COLS), f32),
        'w0_decay': -1.5 + nrm((L, RWKV_WIDTH), 0.5),
        'w2_decay': nrm((L, DECAY_LORA, RWKV_WIDTH), 0.1),
        'a0': nrm((L, RWKV_WIDTH), 0.5),
        'w2_iclr': nrm((L, ICLR_LORA, RWKV_WIDTH), ICLR_LORA ** -0.5),
        'w2_gate': nrm((L, GATE_LORA, RWKV_WIDTH), GATE_LORA ** -0.5),
        'k_k': 0.85 + nrm((L, RWKV_WIDTH), 0.05),
        'k_a': 1.0 + nrm((L, RWKV_WIDTH), 0.05),
        'r_k': nrm((L, RWKV_HEADS, RWKV_HEAD_DIM), 0.1),
        'lnx_g': 1.0 + nrm((L, RWKV_WIDTH), 0.02),
        'lnx_b': nrm((L, RWKV_WIDTH), 0.02),
        'conv_w': nrm((L, CONV_TAPS, CONV_CH), CONV_TAPS ** -0.5),
        'b_forget': 3.0 + nrm((L, FOX_HEADS), 1.0),
        'b_gate': nrm((L, N_BRANCHES, D), 0.1),
        'w_up_rwkv': nrm((L, RWKV_WIDTH, D), RWKV_WIDTH ** -0.5),
        'w_up_conv': nrm((L, CONV_CH, D), CONV_CH ** -0.5),
        'w_up_attn': nrm((L, FOX_WIDTH, D), FOX_WIDTH ** -0.5),
        'w_out': nrm((L, D, D), D ** -0.5 * DEEPNORM_BETA),
        'ln1_g': 1.0 + nrm((L, D), 0.02),
        'ln1_b': nrm((L, D), 0.02),
        'ln2_g': 1.0 + nrm((L, D), 0.02),
        'ln2_b': nrm((L, D), 0.02),
        'ffn_w1': nrm((N_DENSE, D, D_FF), D ** -0.5),
        'ffn_w3': nrm((N_DENSE, D, D_FF), D ** -0.5),
        'ffn_w2': nrm((N_DENSE, D_FF, D), D_FF ** -0.5 * DEEPNORM_BETA),
        'router_w': nrm((N_MOE, D, N_EXPERTS), D ** -0.5),
        'router_b': nrm((N_MOE, N_EXPERTS), 0.01),
        'moe_w1': nrm((N_MOE, N_EXPERTS, D, D_FF_EXPERT), D ** -0.5),
        'moe_w3': nrm((N_MOE, N_EXPERTS, D, D_FF_EXPERT), D ** -0.5),
        'moe_w2': nrm((N_MOE, N_EXPERTS, D_FF_EXPERT, D), D_FF_EXPERT ** -0.5 * DEEPNORM_BETA),
    }


def reference(x, ln0_g, ln0_b, w_in, mu_shift, w0_decay, w2_decay, a0, w2_iclr, w2_gate, k_k, k_a, r_k,
              lnx_g, lnx_b, conv_w, b_forget, b_gate, w_up_rwkv, w_up_conv, w_up_attn, w_out,
              ln1_g, ln1_b, ln2_g, ln2_b, ffn_w1, ffn_w3, ffn_w2, router_w, router_b,
              moe_w1, moe_w3, moe_w2):
    x = layer_norm(x, ln0_g, ln0_b)
    for l in range(DEPTH):
        mix = hybrid_token_mixer(x, w_in[l], mu_shift[l], w0_decay[l], w2_decay[l], a0[l], w2_iclr[l],
                                 w2_gate[l], k_k[l], k_a[l], r_k[l], lnx_g[l], lnx_b[l], conv_w[l],
                                 b_forget[l], b_gate[l], w_up_rwkv[l], w_up_conv[l], w_up_attn[l], w_out[l])
        x = layer_norm(DEEPNORM_ALPHA * x + mix, ln1_g[l], ln1_b[l])
        i = l // 2
        if l % 2 == 0:
            ffn = swiglu(x, ffn_w1[i], ffn_w3[i], ffn_w2[i])
        else:
            ffn = moe_swiglu(x, router_w[i], router_b[i], moe_w1[i], moe_w3[i], moe_w2[i])
        x = layer_norm(DEEPNORM_ALPHA * x + ffn, ln2_g[l], ln2_b[l])
    return x
```

```python
import functools
import math

import jax
import jax.numpy as jnp
from jax import lax
from jax.experimental import pallas as pl
from jax.experimental.pallas import tpu as pltpu

F32 = jnp.float32
BF16 = jnp.bfloat16

D_MODEL = 1024
BRANCH = 512
HEAD_DIM = 64
N_HEADS = BRANCH // HEAD_DIM
LORA_W = 256
N_EXPERTS = 8
LANES = 128
RWKV_LN_EPS = 64e-5
LN_EPS = 1e-5
DECAY_SCALE = math.exp(-0.5)

COL_R, COL_K, COL_V, COL_LORA = 0, 512, 1024, 1536
COL_CB, COL_CC, COL_CH = 2048, 2560, 3072
COL_FQ, COL_FK, COL_FV = 3584, 4096, 4608
COL_GATE = 5120
PROJ_COLS = 8192

RWKV_CHUNK = 64
VMEM_LIMIT = 56 * 1024 * 1024


def _cparams(sem):
    return pltpu.CompilerParams(dimension_semantics=sem, vmem_limit_bytes=VMEM_LIMIT)


def _dot(a, b):
    return jnp.dot(a.astype(BF16), b.astype(BF16), preferred_element_type=F32)


def _dot_nt(a, b):
    return lax.dot_general(a.astype(BF16), b.astype(BF16), (((1,), (1,)), ((), ())),
                           preferred_element_type=F32)


def _split2(x):
    hi = x.astype(BF16)
    lo = (x - hi.astype(F32)).astype(BF16)
    return hi, lo


def _split3(x):
    x1 = x.astype(BF16)
    r1 = x - x1.astype(F32)
    x2 = r1.astype(BF16)
    x3 = (r1 - x2.astype(F32)).astype(BF16)
    return x1, x2, x3


def _layer_norm(x, g, b):
    mu = jnp.mean(x, axis=-1, keepdims=True)
    xc = x - mu
    var = jnp.mean(xc * xc, axis=-1, keepdims=True)
    return xc * lax.rsqrt(var + LN_EPS) * g + b


def _sigmoid(x):
    return 1.0 / (1.0 + jnp.exp(-x))


def _ln0_kernel(x_ref, g_ref, b_ref, o_ref):
    o_ref[...] = _layer_norm(x_ref[...], g_ref[...], b_ref[...])


def _ln0(x2d, g, b, tm=1024):
    n = x2d.shape[0]
    tm = min(tm, n)
    return pl.pallas_call(
        _ln0_kernel,
        grid=(n // tm,),
        in_specs=[pl.BlockSpec((tm, D_MODEL), lambda i: (i, 0)),
                  pl.BlockSpec((1, D_MODEL), lambda i: (0, 0)),
                  pl.BlockSpec((1, D_MODEL), lambda i: (0, 0))],
        out_specs=pl.BlockSpec((tm, D_MODEL), lambda i: (i, 0)),
        out_shape=jax.ShapeDtypeStruct((n, D_MODEL), F32),
        compiler_params=_cparams(("parallel",)),
        name="ln0",
    )(x2d, g.reshape(1, -1), b.reshape(1, -1))


def _proj_kernel(x_ref, w_ref, o_ref, xb_ref):
    @pl.when(pl.program_id(1) == 0)
    def _():
        xb_ref[...] = x_ref[...].astype(BF16)

    o_ref[...] = jnp.dot(xb_ref[...], w_ref[...], preferred_element_type=F32).astype(o_ref.dtype)


def _proj(x2d, w, tm=1024, tn=1024):
    n = x2d.shape[0]
    cols = w.shape[1]
    tm = min(tm, n)
    return pl.pallas_call(
        _proj_kernel,
        grid=(n // tm, cols // tn),
        in_specs=[pl.BlockSpec((tm, D_MODEL), lambda i, j: (i, 0)),
                  pl.BlockSpec((D_MODEL, tn), lambda i, j: (0, j))],
        out_specs=pl.BlockSpec((tm, tn), lambda i, j: (i, j)),
        out_shape=jax.ShapeDtypeStruct((n, cols), BF16),
        scratch_shapes=[pltpu.VMEM((tm, D_MODEL), BF16)],
        compiler_params=_cparams(("parallel", "arbitrary")),
        name="in_proj",
    )(x2d, w)


def _shifted(x, prev_row):
    rolled = pltpu.roll(x, 1, 0)
    row = lax.broadcasted_iota(jnp.int32, x.shape, 0)
    return jnp.where(row == 0, prev_row, rolled)


def _rwkv_kernel(r_ref, k_ref, v_ref, lo_ref, mur_ref, muk_ref, muv_ref, mulo_ref,
                 w0_ref, w2d_ref, a0_ref, w2i_ref, w2g_ref, kk_ref, ka_ref, rk_ref,
                 lng_ref, lnb_ref, o_ref,
                 prev_ref, prevlo_ref, h_ref, r_s, lw_s, k_s, v_s, a_s, b_s, y_s):
    tt = r_ref.shape[1]
    c = RWKV_CHUNK

    @pl.when(pl.program_id(1) == 0)
    def _():
        prev_ref[...] = jnp.zeros_like(prev_ref)
        prevlo_ref[...] = jnp.zeros_like(prevlo_ref)
        h_ref[...] = jnp.zeros_like(h_ref)

    def lerp(ref, mu_ref, p_ref, idx, width):
        x = ref[0].astype(F32)
        prev_row = p_ref[0:1, idx * BRANCH: idx * BRANCH + width]
        out = x + mu_ref[...] * (_shifted(x, prev_row) - x)
        p_ref[0:1, idx * BRANCH: idx * BRANCH + width] = x[tt - 1:tt, :]
        return out

    r = lerp(r_ref, mur_ref, prev_ref, 0, BRANCH)
    k = lerp(k_ref, muk_ref, prev_ref, 1, BRANCH)
    v = lerp(v_ref, muv_ref, prev_ref, 2, BRANCH)
    lo = lerp(lo_ref, mulo_ref, prevlo_ref, 0, LORA_W)

    w_raw = w0_ref[...] + _dot(jnp.tanh(lo[:, 0:64]), w2d_ref[...])
    lw = -DECAY_SCALE * _sigmoid(w_raw)
    a = _sigmoid(a0_ref[...] + _dot(lo[:, 64:128], w2i_ref[...]))
    g = _dot(_sigmoid(lo[:, 128:256]), w2g_ref[...])

    rowh = lax.broadcasted_iota(jnp.int32, (BRANCH, BRANCH), 0) // HEAD_DIM
    colh = lax.broadcasted_iota(jnp.int32, (BRANCH, BRANCH), 1) // HEAD_DIM
    head_ones = jnp.where(rowh == colh, 1.0, 0.0).astype(BF16)

    def head_sum(x):
        hi, lo_ = _split2(x)
        return (jnp.dot(hi, head_ones, preferred_element_type=F32)
                + jnp.dot(lo_, head_ones, preferred_element_type=F32))

    kk = k * kk_ref[...]
    kk_norm = jnp.sqrt(head_sum(kk * kk))
    kkn = kk / jnp.maximum(kk_norm, 1e-12)
    k2 = k * (1.0 + (a - 1.0) * ka_ref[...])
    bonus = head_sum(r * k2 * rk_ref[...]) * v

    r_s[...] = r
    lw_s[...] = lw
    k_s[...] = k2
    v_s[...] = v
    a_s[...] = -kkn
    b_s[...] = kkn * a

    ri = lax.broadcasted_iota(jnp.int32, (c, c), 0)
    ci = lax.broadcasted_iota(jnp.int32, (c, c), 1)
    tri_incl = ri >= ci
    tri_strict = ri > ci
    eye = ri == ci
    ltri = jnp.where(tri_incl, 1.0, 0.0).astype(BF16)
    eye_f = jnp.where(eye, 1.0, 0.0).astype(F32)

    def chunk_body(ic, carry):
        rows = pl.ds(pl.multiple_of(ic * c, c), c)
        lw_c = lw_s[rows, :]
        lw_hi, lw_lo = _split2(lw_c)
        cum = (jnp.dot(ltri, lw_hi, preferred_element_type=F32)
               + jnp.dot(ltri, lw_lo, preferred_element_type=F32))
        cum_last = cum[c - 1:c, :]
        p_incl = jnp.exp(cum)
        p_inv = jnp.exp(-cum)
        p_prev = jnp.exp(cum - lw_c)
        p_tail = jnp.exp(cum_last - cum)
        p_last = jnp.exp(cum_last)
        rt_all = r_s[rows, :] * p_incl
        at_all = a_s[rows, :] * p_prev
        bt_all = b_s[rows, :] * p_inv
        kt_all = k_s[rows, :] * p_inv
        bh_all = b_s[rows, :] * p_tail
        kh_all = k_s[rows, :] * p_tail
        v_all = v_s[rows, :]
        ys = []
        for h in range(N_HEADS):
            sl = slice(h * HEAD_DIM, (h + 1) * HEAD_DIM)
            at, rt, bt, kt = at_all[:, sl], rt_all[:, sl], bt_all[:, sl], kt_all[:, sl]
            vh = v_all[:, sl]
            aa = _dot_nt(jnp.concatenate([at, rt], axis=0), jnp.concatenate([bt, kt], axis=0))
            n_ab = jnp.where(tri_strict, aa[0:c, 0:c], 0.0)
            a_ak = jnp.where(tri_strict, aa[0:c, c:2 * c], 0.0)
            a_rb = jnp.where(tri_incl, aa[c:2 * c, 0:c], 0.0)
            a_rk = jnp.where(tri_incl, aa[c:2 * c, c:2 * c], 0.0)
            tinv = eye_f + n_ab
            npow = n_ab
            for _ in range(5):
                npow = _dot(npow, npow)
                tinv = tinv + _dot(npow, tinv)
            h0 = h_ref[h]
            ar_h = _dot(jnp.concatenate([at, rt], axis=0), h0)
            akv = _dot(jnp.concatenate([a_ak, a_rk], axis=0), vh)
            u = _dot(tinv, ar_h[0:c] + akv[0:c])
            y = ar_h[c:2 * c] + akv[c:2 * c] + _dot(a_rb, u)
            bk_t = _dot_nt(eye_f, jnp.concatenate([bh_all[:, sl], kh_all[:, sl]], axis=0))
            p_col = jnp.sum(jnp.where(eye, p_last[:, sl], 0.0), axis=1, keepdims=True)
            h_ref[h] = p_col * h0 + _dot(bk_t, jnp.concatenate([u, vh], axis=0))
            ys.append(y)
        y_s[rows, :] = jnp.concatenate(ys, axis=1)
        return carry

    lax.fori_loop(0, tt // c, chunk_body, 0)

    y = y_s[...]
    mean = head_sum(y) * (1.0 / HEAD_DIM)
    yc = y - mean
    var = head_sum(yc * yc) * (1.0 / HEAD_DIM)
    yn = yc * lax.rsqrt(var + RWKV_LN_EPS) * lng_ref[...] + lnb_ref[...]
    o_ref[0] = ((yn + bonus) * g).astype(o_ref.dtype)


def _rwkv(proj3, mu_shift, w0, w2d, a0, w2i, w2g, k_k, k_a, r_k, lnx_g, lnx_b, tt=512):
    bsz, t_len, _ = proj3.shape
    tt = min(tt, t_len)
    row = lambda a: a.reshape(1, -1).astype(F32)
    mu_r, mu_k, mu_v = (row(mu_shift[i * BRANCH:(i + 1) * BRANCH]) for i in range(3))
    mu_lo = row(mu_shift[3 * BRANCH:])
    vec = lambda width: pl.BlockSpec((1, width), lambda b, t: (0, 0))
    mat = lambda shape: pl.BlockSpec(shape, lambda b, t: (0, 0))
    act = lambda col, width: pl.BlockSpec((1, tt, width), lambda b, t: (b, t, col // width))
    return pl.pallas_call(
        _rwkv_kernel,
        grid=(bsz, t_len // tt),
        in_specs=[act(COL_R, BRANCH), act(COL_K, BRANCH), act(COL_V, BRANCH), act(COL_LORA, LORA_W),
                  vec(BRANCH), vec(BRANCH), vec(BRANCH), vec(LORA_W),
                  vec(BRANCH), mat((64, BRANCH)), vec(BRANCH), mat((64, BRANCH)), mat((128, BRANCH)),
                  vec(BRANCH), vec(BRANCH), vec(BRANCH), vec(BRANCH), vec(BRANCH)],
        out_specs=pl.BlockSpec((1, tt, BRANCH), lambda b, t: (b, t, 0)),
        out_shape=jax.ShapeDtypeStruct((bsz, t_len, BRANCH), BF16),
        scratch_shapes=[pltpu.VMEM((8, 3 * BRANCH), F32), pltpu.VMEM((8, LORA_W), F32),
                        pltpu.VMEM((N_HEADS, HEAD_DIM, HEAD_DIM), F32)]
                       + [pltpu.VMEM((tt, BRANCH), F32) for _ in range(7)],
        compiler_params=_cparams(("parallel", "arbitrary")),
        name="rwkv7",
    )(proj3, proj3, proj3, proj3, mu_r, mu_k, mu_v, mu_lo,
      row(w0), w2d.astype(BF16), row(a0), w2i.astype(BF16), w2g.astype(BF16),
      row(k_k), row(k_a), row(r_k), row(lnx_g), row(lnx_b))


def _forget_kernel(x_ref, wf_ref, bf_ref, nat_ref, row_ref, carry_ref):
    tt = x_ref.shape[1]

    @pl.when(pl.program_id(1) == 0)
    def _():
        carry_ref[...] = jnp.zeros_like(carry_ref)

    f = jnp.dot(x_ref[0].astype(BF16), wf_ref[...], preferred_element_type=F32) + bf_ref[...]
    log_f = jnp.minimum(f, 0.0) - jnp.log1p(jnp.exp(-jnp.abs(f)))
    ri = lax.broadcasted_iota(jnp.int32, (tt, tt), 0)
    ci = lax.broadcasted_iota(jnp.int32, (tt, tt), 1)
    ltri = jnp.where(ri >= ci, 1.0, 0.0).astype(BF16)
    cum = carry_ref[0:1, :]
    for part in _split3(log_f):
        cum = cum + jnp.dot(ltri, part, preferred_element_type=F32)
    carry_ref[0:1, :] = cum[tt - 1:tt, :]
    nat_ref[0] = cum
    sel = jnp.where(lax.broadcasted_iota(jnp.int32, (N_HEADS, LANES), 0)
                    == lax.broadcasted_iota(jnp.int32, (N_HEADS, LANES), 1), 1.0, 0.0).astype(BF16)
    rows = jnp.zeros((N_HEADS, tt), F32)
    for part in _split3(cum):
        rows = rows + lax.dot_general(sel, part, (((1,), (1,)), ((), ())), preferred_element_type=F32)
    row_ref[0] = rows


def _forget_cumsum(x3, w_f, b_forget, tt=512):
    bsz, t_len, _ = x3.shape
    tt = min(tt, t_len)
    wf = jnp.zeros((D_MODEL, LANES), BF16).at[:, :N_HEADS].set(w_f.astype(BF16))
    bf = jnp.zeros((1, LANES), F32).at[0, :N_HEADS].set(b_forget)
    return pl.pallas_call(
        _forget_kernel,
        grid=(bsz, t_len // tt),
        in_specs=[pl.BlockSpec((1, tt, D_MODEL), lambda b, t: (b, t, 0)),
                  pl.BlockSpec((D_MODEL, LANES), lambda b, t: (0, 0)),
                  pl.BlockSpec((1, LANES), lambda b, t: (0, 0))],
        out_specs=[pl.BlockSpec((1, tt, LANES), lambda b, t: (b, t, 0)),
                   pl.BlockSpec((1, N_HEADS, tt), lambda b, t: (b, 0, t))],
        out_shape=[jax.ShapeDtypeStruct((bsz, t_len, LANES), F32),
                   jax.ShapeDtypeStruct((bsz, N_HEADS, t_len), F32)],
        scratch_shapes=[pltpu.VMEM((8, LANES), F32)],
        compiler_params=_cparams(("parallel", "arbitrary")),
        name="forget_cumsum",
    )(x3, wf, bf)


def _fox_kernel(q_ref, k_ref, v_ref, cnat_ref, crow_ref, o_ref, *, tq):
    hp = pl.program_id(1)
    i = pl.program_id(2)
    scale = HEAD_DIM ** -0.5
    q = q_ref[0].astype(F32) * scale
    lane = lax.broadcasted_iota(jnp.int32, (tq, LANES), 1)
    cnat = cnat_ref[0]
    ri = lax.broadcasted_iota(jnp.int32, (tq, tq), 0)
    ci = lax.broadcasted_iota(jnp.int32, (tq, tq), 1)
    causal = ci <= ri
    outs = []
    for hh in range(2):
        in_head = (lane >= hh * HEAD_DIM) & (lane < (hh + 1) * HEAD_DIM)
        qh = jnp.where(in_head, q, 0.0).astype(BF16)
        head = hp * 2 + hh
        cq = jnp.sum(jnp.where(lane == head, cnat, 0.0), axis=1, keepdims=True)

        def scores(j):
            cols = pl.ds(pl.multiple_of(j * tq, tq), tq)
            s = _dot_nt(qh, k_ref[0, cols, :])
            ck = crow_ref[0, pl.ds(head, 1), cols]
            return s + cq - ck, cols

        def update(s, cols, m, l, acc):
            m_new = jnp.maximum(m, jnp.max(s, axis=1, keepdims=True))
            alpha = jnp.exp(m - m_new)
            p = jnp.exp(s - m_new)
            l_new = alpha * l + jnp.sum(p, axis=1, keepdims=True)
            acc_new = alpha * acc + jnp.dot(p.astype(BF16), v_ref[0, cols, :],
                                            preferred_element_type=F32)
            return m_new, l_new, acc_new

        def body(j, carry):
            s, cols = scores(j)
            return update(s, cols, *carry)

        init = (jnp.full((tq, 1), -1e30, F32), jnp.zeros((tq, 1), F32), jnp.zeros((tq, LANES), F32))
        m, l, acc = lax.fori_loop(0, i, body, init)
        s, cols = scores(i)
        s = jnp.where(causal, s, -1e30)
        m, l, acc = update(s, cols, m, l, acc)
        outs.append(acc / l)
    o_ref[0] = jnp.where(lane < HEAD_DIM, outs[0], outs[1]).astype(o_ref.dtype)


def _fox_attention(proj3, cum_nat, cum_row, tq=256):
    bsz, t_len, _ = proj3.shape
    tq = min(tq, t_len)
    qcol, kcol, vcol = COL_FQ // LANES, COL_FK // LANES, COL_FV // LANES
    return pl.pallas_call(
        functools.partial(_fox_kernel, tq=tq),
        grid=(bsz, N_HEADS // 2, t_len // tq),
        in_specs=[pl.BlockSpec((1, tq, LANES), lambda b, hp, i: (b, i, qcol + hp)),
                  pl.BlockSpec((1, t_len, LANES), lambda b, hp, i: (b, 0, kcol + hp)),
                  pl.BlockSpec((1, t_len, LANES), lambda b, hp, i: (b, 0, vcol + hp)),
                  pl.BlockSpec((1, tq, LANES), lambda b, hp, i: (b, i, 0)),
                  pl.BlockSpec((1, N_HEADS, t_len), lambda b, hp, i: (b, 0, 0))],
        out_specs=pl.BlockSpec((1, tq, LANES), lambda b, hp, i: (b, i, hp)),
        out_shape=jax.ShapeDtypeStruct((bsz, t_len, BRANCH), BF16),
        compiler_params=_cparams(("parallel", "parallel", "arbitrary")),
        name="fox_attention",
    )(proj3, proj3, proj3, cum_nat, cum_row)


def _merge_kernel(oa_ref, cb_ref, cc_ref, ch_ref, ccp_ref, chp_ref, oc_ref,
                  g0_ref, g1_ref, g2_ref, x_ref, convw_ref, bg_ref,
                  wua_ref, wub_ref, wuc_ref, wo_ref, lg_ref, lb_ref, o_ref, *, alpha, tiles_per_seq):
    tm = x_ref.shape[0]
    first = (pl.program_id(0) % tiles_per_seq) == 0
    u = cc_ref[...].astype(F32) * ch_ref[...].astype(F32)
    u_prev = ccp_ref[...].astype(F32) * chp_ref[...].astype(F32)
    u_prev = jnp.where(first, 0.0, u_prev)
    row = lax.broadcasted_iota(jnp.int32, u.shape, 0)
    u1 = jnp.where(row == 0, u_prev[7:8, :], pltpu.roll(u, 1, 0))
    u2 = pltpu.roll(u, 2, 0)
    u2 = jnp.where(row == 0, u_prev[6:7, :], jnp.where(row == 1, u_prev[7:8, :], u2))
    cw = convw_ref[...]
    conv = cw[0:1, :] * u2 + cw[1:2, :] * u1 + cw[2:3, :] * u
    o_b = cb_ref[...].astype(F32) * conv

    bg = bg_ref[...]
    merged = _sigmoid(g0_ref[...].astype(F32) + bg[0:1, :]) * jnp.dot(
        oa_ref[...], wua_ref[...], preferred_element_type=F32)
    merged = merged + _sigmoid(g1_ref[...].astype(F32) + bg[1:2, :]) * _dot(o_b, wub_ref[...])
    merged = merged + _sigmoid(g2_ref[...].astype(F32) + bg[2:3, :]) * jnp.dot(
        oc_ref[...], wuc_ref[...], preferred_element_type=F32)
    mix = _dot(merged, wo_ref[...])
    o_ref[...] = _layer_norm(alpha * x_ref[...] + mix, lg_ref[...], lb_ref[...])


def _merge(o_a, proj, o_c, x2d, conv_w, b_gate, wua, wub, wuc, wo, ln_g, ln_b, alpha, t_len, tm=512):
    n = x2d.shape[0]
    tm = min(tm, t_len)
    tiles_per_seq = t_len // tm
    act = lambda col, width: pl.BlockSpec((tm, width), lambda i: (i, col // width))
    halo = lambda col: pl.BlockSpec(
        (8, BRANCH), lambda i: (jnp.maximum(i * (tm // 8) - 1, 0), col // BRANCH))
    full = lambda shape: pl.BlockSpec(shape, lambda i: (0, 0))
    row = lambda a: a.reshape(1, -1).astype(F32)
    return pl.pallas_call(
        functools.partial(_merge_kernel, alpha=alpha, tiles_per_seq=tiles_per_seq),
        grid=(n // tm,),
        in_specs=[act(0, BRANCH), act(COL_CB, BRANCH), act(COL_CC, BRANCH), act(COL_CH, BRANCH),
                  halo(COL_CC), halo(COL_CH), act(0, BRANCH),
                  act(COL_GATE, D_MODEL), act(COL_GATE + D_MODEL, D_MODEL),
                  act(COL_GATE + 2 * D_MODEL, D_MODEL),
                  act(0, D_MODEL), full((3, BRANCH)), full((3, D_MODEL)),
                  full((BRANCH, D_MODEL)), full((BRANCH, D_MODEL)), full((BRANCH, D_MODEL)),
                  full((D_MODEL, D_MODEL)), full((1, D_MODEL)), full((1, D_MODEL))],
        out_specs=pl.BlockSpec((tm, D_MODEL), lambda i: (i, 0)),
        out_shape=jax.ShapeDtypeStruct((n, D_MODEL), F32),
        compiler_params=_cparams(("parallel",)),
        name="merge",
    )(o_a, proj, proj, proj, proj, proj, o_c, proj, proj, proj, x2d,
      conv_w.astype(F32), b_gate.astype(F32),
      wua.astype(BF16), wub.astype(BF16), wuc.astype(BF16), wo.astype(BF16), row(ln_g), row(ln_b))


def _ffn_kernel(x_ref, w1_ref, w3_ref, w2_ref, lg_ref, lb_ref, o_ref, xb_ref, acc_ref, *, alpha):
    j = pl.program_id(1)

    @pl.when(j == 0)
    def _():
        xb_ref[...] = x_ref[...].astype(BF16)
        acc_ref[...] = jnp.zeros_like(acc_ref)

    xb = xb_ref[...]
    h1 = jnp.dot(xb, w1_ref[...], preferred_element_type=F32)
    h3 = jnp.dot(xb, w3_ref[...], preferred_element_type=F32)
    hidden = (h1 * _sigmoid(h1) * h3).astype(BF16)
    acc_ref[...] += jnp.dot(hidden, w2_ref[...], preferred_element_type=F32)

    @pl.when(j == pl.num_programs(1) - 1)
    def _():
        o_ref[...] = _layer_norm(alpha * x_ref[...] + acc_ref[...], lg_ref[...], lb_ref[...])


def _ffn_dense(x2d, w1, w3, w2, ln_g, ln_b, alpha, tm=1024, tf=256):
    n = x2d.shape[0]
    tm = min(tm, n)
    d_ff = w1.shape[1]
    row = lambda a: a.reshape(1, -1).astype(F32)
    return pl.pallas_call(
        functools.partial(_ffn_kernel, alpha=alpha),
        grid=(n // tm, d_ff // tf),
        in_specs=[pl.BlockSpec((tm, D_MODEL), lambda i, j: (i, 0)),
                  pl.BlockSpec((D_MODEL, tf), lambda i, j: (0, j)),
                  pl.BlockSpec((D_MODEL, tf), lambda i, j: (0, j)),
                  pl.BlockSpec((tf, D_MODEL), lambda i, j: (j, 0)),
                  pl.BlockSpec((1, D_MODEL), lambda i, j: (0, 0)),
                  pl.BlockSpec((1, D_MODEL), lambda i, j: (0, 0))],
        out_specs=pl.BlockSpec((tm, D_MODEL), lambda i, j: (i, 0)),
        out_shape=jax.ShapeDtypeStruct((n, D_MODEL), F32),
        scratch_shapes=[pltpu.VMEM((tm, D_MODEL), BF16), pltpu.VMEM((tm, D_MODEL), F32)],
        compiler_params=_cparams(("parallel", "arbitrary")),
        name="ffn_dense",
    )(x2d, w1.astype(BF16), w3.astype(BF16), w2.astype(BF16), row(ln_g), row(ln_b))


def _router_kernel(x_ref, w_ref, b_ref, comb_ref):
    x = x_ref[...]
    tm = x.shape[0]
    logits = b_ref[...] + jnp.zeros((tm, LANES), F32)
    w_parts = (w_ref[0], w_ref[1], w_ref[2])
    x_parts = _split3(x)
    for xi in range(3):
        for wi in range(3 - xi):
            logits = logits + jnp.dot(x_parts[xi], w_parts[wi], preferred_element_type=F32)
    lane = lax.broadcasted_iota(jnp.int32, (tm, LANES), 1)
    neg = -1e30
    logits = jnp.where(lane < N_EXPERTS, logits, neg)
    m1 = jnp.max(logits, axis=1, keepdims=True)
    i1 = jnp.min(jnp.where(logits == m1, lane, LANES), axis=1, keepdims=True)
    rest = jnp.where(lane == i1, neg, logits)
    m2 = jnp.max(rest, axis=1, keepdims=True)
    i2 = jnp.min(jnp.where(rest == m2, lane, LANES), axis=1, keepdims=True)
    e2 = jnp.exp(m2 - m1)
    g1 = 1.0 / (1.0 + e2)
    g2 = e2 / (1.0 + e2)
    comb_ref[...] = jnp.where(lane == i1, g1, jnp.where(lane == i2, g2, 0.0))


def _router(x2d, router_w, router_b, tm=1024):
    n = x2d.shape[0]
    tm = min(tm, n)
    w_pad = jnp.zeros((D_MODEL, LANES), F32).at[:, :N_EXPERTS].set(router_w)
    w_parts = jnp.stack(_split3(w_pad))
    b_pad = jnp.zeros((1, LANES), F32).at[0, :N_EXPERTS].set(router_b)
    return pl.pallas_call(
        _router_kernel,
        grid=(n // tm,),
        in_specs=[pl.BlockSpec((tm, D_MODEL), lambda i: (i, 0)),
                  pl.BlockSpec((3, D_MODEL, LANES), lambda i: (0, 0, 0)),
                  pl.BlockSpec((1, LANES), lambda i: (0, 0))],
        out_specs=pl.BlockSpec((tm, LANES), lambda i: (i, 0)),
        out_shape=jax.ShapeDtypeStruct((n, LANES), F32),
        compiler_params=_cparams(("parallel",)),
        name="router",
    )(x2d, w_parts, b_pad)


def _moe_kernel(x_ref, comb_ref, w1_ref, w3_ref, w2_ref, lg_ref, lb_ref, o_ref, xb_ref, acc_ref, *, alpha):
    e = pl.program_id(1)
    j = pl.program_id(2)

    @pl.when((e == 0) & (j == 0))
    def _():
        xb_ref[...] = x_ref[...].astype(BF16)
        acc_ref[...] = jnp.zeros_like(acc_ref)

    xb = xb_ref[...]
    lane = lax.broadcasted_iota(jnp.int32, comb_ref.shape, 1)
    gate = jnp.sum(jnp.where(lane == e, comb_ref[...], 0.0), axis=1, keepdims=True)
    h1 = jnp.dot(xb, w1_ref[0], preferred_element_type=F32)
    h3 = jnp.dot(xb, w3_ref[0], preferred_element_type=F32)
    hidden = (gate * (h1 * _sigmoid(h1) * h3)).astype(BF16)
    acc_ref[...] += jnp.dot(hidden, w2_ref[0], preferred_element_type=F32)

    @pl.when((e == pl.num_programs(1) - 1) & (j == pl.num_programs(2) - 1))
    def _():
        o_ref[...] = _layer_norm(alpha * x_ref[...] + acc_ref[...], lg_ref[...], lb_ref[...])


def _ffn_moe(x2d, router_w, router_b, w1, w3, w2, ln_g, ln_b, alpha, tm=1024, tf=512):
    n = x2d.shape[0]
    tm = min(tm, n)
    d_ff = w1.shape[2]
    comb = _router(x2d, router_w, router_b)
    row = lambda a: a.reshape(1, -1).astype(F32)
    return pl.pallas_call(
        functools.partial(_moe_kernel, alpha=alpha),
        grid=(n // tm, N_EXPERTS, d_ff // tf),
        in_specs=[pl.BlockSpec((tm, D_MODEL), lambda i, e, j: (i, 0)),
                  pl.BlockSpec((tm, LANES), lambda i, e, j: (i, 0)),
                  pl.BlockSpec((1, D_MODEL, tf), lambda i, e, j: (e, 0, j)),
                  pl.BlockSpec((1, D_MODEL, tf), lambda i, e, j: (e, 0, j)),
                  pl.BlockSpec((1, tf, D_MODEL), lambda i, e, j: (e, j, 0)),
                  pl.BlockSpec((1, D_MODEL), lambda i, e, j: (0, 0)),
                  pl.BlockSpec((1, D_MODEL), lambda i, e, j: (0, 0))],
        out_specs=pl.BlockSpec((tm, D_MODEL), lambda i, e, j: (i, 0)),
        out_shape=jax.ShapeDtypeStruct((n, D_MODEL), F32),
        scratch_shapes=[pltpu.VMEM((tm, D_MODEL), BF16), pltpu.VMEM((tm, D_MODEL), F32)],
        compiler_params=_cparams(("parallel", "arbitrary", "arbitrary")),
        name="ffn_moe",
    )(x2d, comb, w1.astype(BF16), w3.astype(BF16), w2.astype(BF16), row(ln_g), row(ln_b))


def _arrange_w_in(w_in_l):
    rwkv_cols = 3 * BRANCH + LORA_W
    conv0 = rwkv_cols
    fox0 = conv0 + 3 * BRANCH
    f0 = fox0 + 3 * BRANCH
    gate0 = f0 + N_HEADS
    zeros = jnp.zeros((D_MODEL, COL_CB - COL_LORA - LORA_W), w_in_l.dtype)
    w_all = jnp.concatenate([w_in_l[:, :rwkv_cols], zeros, w_in_l[:, conv0:f0], w_in_l[:, gate0:]], axis=1)
    return w_all.astype(BF16), w_in_l[:, f0:gate0]


def kernel(x, ln0_g, ln0_b, w_in, mu_shift, w0_decay, w2_decay, a0, w2_iclr, w2_gate, k_k, k_a, r_k, lnx_g, lnx_b, conv_w, b_forget, b_gate, w_up_rwkv, w_up_conv, w_up_attn, w_out, ln1_g, ln1_b, ln2_g, ln2_b, ffn_w1, ffn_w3, ffn_w2, router_w, router_b, moe_w1, moe_w3, moe_w2):
    bsz, t_len, _ = x.shape
    n = bsz * t_len
    depth = w_in.shape[0]
    alpha = (2 * depth) ** 0.25
    xs = _ln0(x.reshape(n, D_MODEL), ln0_g, ln0_b)
    for l in range(depth):
        w_all, w_f = _arrange_w_in(w_in[l])
        proj = _proj(xs, w_all)
        proj3 = proj.reshape(bsz, t_len, PROJ_COLS)
        o_a = _rwkv(proj3, mu_shift[l], w0_decay[l], w2_decay[l], a0[l], w2_iclr[l], w2_gate[l],
                    k_k[l], k_a[l], r_k[l].reshape(-1), lnx_g[l], lnx_b[l])
        cum_nat, cum_row = _forget_cumsum(xs.reshape(bsz, t_len, D_MODEL), w_f, b_forget[l])
        o_c = _fox_attention(proj3, cum_nat, cum_row)
        xs = _merge(o_a.reshape(n, BRANCH), proj, o_c.reshape(n, BRANCH), xs, conv_w[l], b_gate[l],
                    w_up_rwkv[l], w_up_conv[l], w_up_attn[l], w_out[l], ln1_g[l], ln1_b[l], alpha, t_len)
        i = l // 2
        if l % 2 == 0:
            xs = _ffn_dense(xs, ffn_w1[i], ffn_w3[i], ffn_w2[i], ln2_g[l], ln2_b[l], alpha)
        else:
            xs = _ffn_moe(xs, router_w[i], router_b[i], moe_w1[i], moe_w3[i], moe_w2[i],
                          ln2_g[l], ln2_b[l], alpha)
    return xs.reshape(bsz, t_len, D_MODEL)
```

```python
import functools
import math

import jax
import jax.numpy as jnp
from jax import lax
from jax.experimental import pallas as pl
from jax.experimental.pallas import tpu as pltpu

F32 = jnp.float32
BF16 = jnp.bfloat16

D_MODEL = 1024
BRANCH = 512
HEAD_DIM = 64
N_HEADS = BRANCH // HEAD_DIM
LORA_W = 256
N_EXPERTS = 8
LANES = 128
RWKV_LN_EPS = 64e-5
LN_EPS = 1e-5
DECAY_SCALE = math.exp(-0.5)

COL_R, COL_K, COL_V, COL_LORA = 0, 512, 1024, 1536
COL_CB, COL_CC, COL_CH = 2048, 2560, 3072
COL_FQ, COL_FK, COL_FV = 3584, 4096, 4608
COL_GATE = 5120
PROJ_COLS = 8192

RWKV_CHUNK = 64
GROUP_HEADS = 4
N_GROUPS = N_HEADS // GROUP_HEADS
CHUNKS_PER_ITER = 8
VMEM_LIMIT = 56 * 1024 * 1024


def _cparams(sem):
    return pltpu.CompilerParams(dimension_semantics=sem, vmem_limit_bytes=VMEM_LIMIT)


def _dot(a, b):
    return jnp.dot(a.astype(BF16), b.astype(BF16), preferred_element_type=F32)


def _dot_nt(a, b):
    return lax.dot_general(a.astype(BF16), b.astype(BF16), (((1,), (1,)), ((), ())),
                           preferred_element_type=F32)


def _split2(x):
    hi = x.astype(BF16)
    lo = (x - hi.astype(F32)).astype(BF16)
    return hi, lo


def _split3(x):
    x1 = x.astype(BF16)
    r1 = x - x1.astype(F32)
    x2 = r1.astype(BF16)
    x3 = (r1 - x2.astype(F32)).astype(BF16)
    return x1, x2, x3


def _layer_norm(x, g, b):
    mu = jnp.mean(x, axis=-1, keepdims=True)
    xc = x - mu
    var = jnp.mean(xc * xc, axis=-1, keepdims=True)
    return xc * lax.rsqrt(var + LN_EPS) * g + b


def _sigmoid(x):
    return 1.0 / (1.0 + jnp.exp(-x))


def _ln0_kernel(x_ref, g_ref, b_ref, o_ref):
    o_ref[...] = _layer_norm(x_ref[...], g_ref[...], b_ref[...])


def _ln0(x2d, g, b, tm=1024):
    n = x2d.shape[0]
    tm = min(tm, n)
    return pl.pallas_call(
        _ln0_kernel,
        grid=(n // tm,),
        in_specs=[pl.BlockSpec((tm, D_MODEL), lambda i: (i, 0)),
                  pl.BlockSpec((1, D_MODEL), lambda i: (0, 0)),
                  pl.BlockSpec((1, D_MODEL), lambda i: (0, 0))],
        out_specs=pl.BlockSpec((tm, D_MODEL), lambda i: (i, 0)),
        out_shape=jax.ShapeDtypeStruct((n, D_MODEL), F32),
        compiler_params=_cparams(("parallel",)),
        name="ln0",
    )(x2d, g.reshape(1, -1), b.reshape(1, -1))


def _proj_kernel(x_ref, w_ref, o_ref, xb_ref):
    @pl.when(pl.program_id(1) == 0)
    def _():
        xb_ref[...] = x_ref[...].astype(BF16)

    o_ref[...] = jnp.dot(xb_ref[...], w_ref[...], preferred_element_type=F32).astype(o_ref.dtype)


def _proj(x2d, w, tm=1024, tn=1024):
    n = x2d.shape[0]
    cols = w.shape[1]
    tm = min(tm, n)
    return pl.pallas_call(
        _proj_kernel,
        grid=(n // tm, cols // tn),
        in_specs=[pl.BlockSpec((tm, D_MODEL), lambda i, j: (i, 0)),
                  pl.BlockSpec((D_MODEL, tn), lambda i, j: (0, j))],
        out_specs=pl.BlockSpec((tm, tn), lambda i, j: (i, j)),
        out_shape=jax.ShapeDtypeStruct((n, cols), BF16),
        scratch_shapes=[pltpu.VMEM((tm, D_MODEL), BF16)],
        compiler_params=_cparams(("parallel", "arbitrary")),
        name="in_proj",
    )(x2d, w)


def _shifted(x, prev_row):
    rolled = pltpu.roll(x, 1, 0)
    row = lax.broadcasted_iota(jnp.int32, x.shape, 0)
    return jnp.where(row == 0, prev_row, rolled)


def _rwkv_kernel(r_ref, k_ref, v_ref, lo_ref, mur_ref, muk_ref, muv_ref, mulo_ref,
                 w0_ref, w2d_ref, a0_ref, w2i_ref, w2g_ref, kk_ref, ka_ref, rk_ref,
                 lng_ref, lnb_ref, o_ref,
                 prev_ref, prevlo_ref, h_ref, r_s, lw_s, k_s, v_s, a_s, b_s, y_s):
    tt = r_ref.shape[1]
    c = RWKV_CHUNK

    @pl.when(pl.program_id(1) == 0)
    def _():
        prev_ref[...] = jnp.zeros_like(prev_ref)
        prevlo_ref[...] = jnp.zeros_like(prevlo_ref)
        h_ref[...] = jnp.zeros_like(h_ref)

    def lerp(ref, mu_ref, p_ref, idx, width):
        x = ref[0].astype(F32)
        prev_row = p_ref[0:1, idx * BRANCH: idx * BRANCH + width]
        out = x + mu_ref[...] * (_shifted(x, prev_row) - x)
        p_ref[0:1, idx * BRANCH: idx * BRANCH + width] = x[tt - 1:tt, :]
        return out

    r = lerp(r_ref, mur_ref, prev_ref, 0, BRANCH)
    k = lerp(k_ref, muk_ref, prev_ref, 1, BRANCH)
    v = lerp(v_ref, muv_ref, prev_ref, 2, BRANCH)
    lo = lerp(lo_ref, mulo_ref, prevlo_ref, 0, LORA_W)

    w_raw = w0_ref[...] + _dot(jnp.tanh(lo[:, 0:64]), w2d_ref[...])
    lw = -DECAY_SCALE * _sigmoid(w_raw)
    a = _sigmoid(a0_ref[...] + _dot(lo[:, 64:128], w2i_ref[...]))
    g = _dot(_sigmoid(lo[:, 128:256]), w2g_ref[...])

    rowh = lax.broadcasted_iota(jnp.int32, (BRANCH, BRANCH), 0) // HEAD_DIM
    colh = lax.broadcasted_iota(jnp.int32, (BRANCH, BRANCH), 1) // HEAD_DIM
    head_ones = jnp.where(rowh == colh, 1.0, 0.0).astype(BF16)

    def head_sum(x):
        hi, lo_ = _split2(x)
        return (jnp.dot(hi, head_ones, preferred_element_type=F32)
                + jnp.dot(lo_, head_ones, preferred_element_type=F32))

    kk = k * kk_ref[...]
    kk_norm = jnp.sqrt(head_sum(kk * kk))
    kkn = kk / jnp.maximum(kk_norm, 1e-12)
    k2 = k * (1.0 + (a - 1.0) * ka_ref[...])
    bonus = head_sum(r * k2 * rk_ref[...]) * v

    r_s[...] = r
    lw_s[...] = lw
    k_s[...] = k2
    v_s[...] = v
    a_s[...] = -kkn
    b_s[...] = kkn * a

    ltri = jnp.where(lax.broadcasted_iota(jnp.int32, (c, c), 0)
                     >= lax.broadcasted_iota(jnp.int32, (c, c), 1), 1.0, 0.0).astype(BF16)

    gw = GROUP_HEADS * HEAD_DIM
    t_w = lax.broadcasted_iota(jnp.int32, (c, gw), 0)
    s_w = lax.broadcasted_iota(jnp.int32, (c, gw), 1) % HEAD_DIM
    strict_w = t_w > s_w
    incl_w = t_w >= s_w
    eye_w = jnp.where(t_w == s_w, 1.0, 0.0)
    eye_wb = eye_w.astype(BF16)
    bd_mask = (lax.broadcasted_iota(jnp.int32, (gw, gw), 0) // HEAD_DIM
               == lax.broadcasted_iota(jnp.int32, (gw, gw), 1) // HEAD_DIM)

    def bd(x):
        xb = x.astype(BF16)
        return jnp.where(bd_mask, jnp.concatenate([xb] * GROUP_HEADS, axis=0), jnp.zeros((), BF16))

    def mm(a, b_bd):
        return jnp.dot(a.astype(BF16), b_bd, preferred_element_type=F32)

    def mm_nt(a, b_bd):
        return lax.dot_general(a.astype(BF16), b_bd, (((1,), (1,)), ((), ())), preferred_element_type=F32)

    def chunk_static(rows, g):
        sl = slice(g * gw, (g + 1) * gw)
        lw_c = lw_s[rows, sl]
        lw_hi, lw_lo = _split2(lw_c)
        cum = (jnp.dot(ltri, lw_hi, preferred_element_type=F32)
               + jnp.dot(ltri, lw_lo, preferred_element_type=F32))
        cum_last = cum[c - 1:c, :]
        p_inv = jnp.exp(-cum)
        p_tail = jnp.exp(cum_last - cum)
        b_c, k_c = b_s[rows, sl], k_s[rows, sl]
        ar = jnp.concatenate([a_s[rows, sl] * jnp.exp(cum - lw_c), r_s[rows, sl] * jnp.exp(cum)],
                             axis=0).astype(BF16)
        yield
        aab = mm_nt(ar, bd(b_c * p_inv))
        aak = mm_nt(ar, bd(k_c * p_inv))
        b_t = mm_nt(eye_wb, bd(b_c * p_tail)).astype(BF16)
        k_t = mm_nt(eye_wb, bd(k_c * p_tail)).astype(BF16)
        pl_hi, pl_lo = _split2(jnp.broadcast_to(jnp.exp(cum_last), (c, gw)))
        decay_w = mm_nt(eye_wb, bd(pl_hi)) + mm_nt(eye_wb, bd(pl_lo))
        yield
        n_w = jnp.where(strict_w, aab[0:c], 0.0)
        a_ark = jnp.concatenate([jnp.where(strict_w, aak[0:c], 0.0),
                                 jnp.where(incl_w, aak[c:2 * c], 0.0)], axis=0).astype(BF16)
        a_rb = jnp.where(incl_w, aab[c:2 * c], 0.0).astype(BF16)
        tinv = eye_w + n_w
        npow = n_w
        for _ in range(5):
            npow = mm(npow, bd(npow))
            yield
            tinv = tinv + mm(npow, bd(tinv))
            yield
        return ar, a_ark, a_rb, tinv.astype(BF16), b_t, k_t, decay_w, bd(v_s[rows, sl])

    def chunk_state(rows, g, st):
        ar, a_ark, a_rb, tinv, b_t, k_t, decay_w, v_bd = st
        h0 = h_ref[g]
        arh = mm(ar, bd(h0))
        akv = mm(a_ark, v_bd)
        yield
        u_bd = bd(mm(tinv, bd(arh[0:c] + akv[0:c])))
        yield
        y_s[rows, g * gw:(g + 1) * gw] = arh[c:2 * c] + akv[c:2 * c] + mm(a_rb, u_bd)
        h_ref[g] = decay_w * h0 + mm(b_t, u_bd) + mm(k_t, v_bd)

    def in_lockstep(gens):
        results = [None] * len(gens)
        live = list(range(len(gens)))
        while live:
            for idx in list(live):
                try:
                    next(gens[idx])
                except StopIteration as stop:
                    results[idx] = stop.value
                    live.remove(idx)
        return results

    def chunk_body(ic, carry):
        items = [(pl.ds(pl.multiple_of((ic * CHUNKS_PER_ITER + j) * c, c), c), g)
                 for j in range(CHUNKS_PER_ITER) for g in range(N_GROUPS)]
        statics = in_lockstep([chunk_static(r, g) for r, g in items])
        for j in range(CHUNKS_PER_ITER):
            in_lockstep([chunk_state(*items[j * N_GROUPS + g], statics[j * N_GROUPS + g])
                         for g in range(N_GROUPS)])
        return carry

    lax.fori_loop(0, tt // (c * CHUNKS_PER_ITER), chunk_body, 0)

    y = y_s[...]
    mean = head_sum(y) * (1.0 / HEAD_DIM)
    yc = y - mean
    var = head_sum(yc * yc) * (1.0 / HEAD_DIM)
    yn = yc * lax.rsqrt(var + RWKV_LN_EPS) * lng_ref[...] + lnb_ref[...]
    o_ref[0] = ((yn + bonus) * g).astype(o_ref.dtype)


def _rwkv(proj3, mu_shift, w0, w2d, a0, w2i, w2g, k_k, k_a, r_k, lnx_g, lnx_b, tt=512):
    bsz, t_len, _ = proj3.shape
    tt = min(tt, t_len)
    row = lambda a: a.reshape(1, -1).astype(F32)
    mu_r, mu_k, mu_v = (row(mu_shift[i * BRANCH:(i + 1) * BRANCH]) for i in range(3))
    mu_lo = row(mu_shift[3 * BRANCH:])
    vec = lambda width: pl.BlockSpec((1, width), lambda b, t: (0, 0))
    mat = lambda shape: pl.BlockSpec(shape, lambda b, t: (0, 0))
    act = lambda col, width: pl.BlockSpec((1, tt, width), lambda b, t: (b, t, col // width))
    return pl.pallas_call(
        _rwkv_kernel,
        grid=(bsz, t_len // tt),
        in_specs=[act(COL_R, BRANCH), act(COL_K, BRANCH), act(COL_V, BRANCH), act(COL_LORA, LORA_W),
                  vec(BRANCH), vec(BRANCH), vec(BRANCH), vec(LORA_W),
                  vec(BRANCH), mat((64, BRANCH)), vec(BRANCH), mat((64, BRANCH)), mat((128, BRANCH)),
                  vec(BRANCH), vec(BRANCH), vec(BRANCH), vec(BRANCH), vec(BRANCH)],
        out_specs=pl.BlockSpec((1, tt, BRANCH), lambda b, t: (b, t, 0)),
        out_shape=jax.ShapeDtypeStruct((bsz, t_len, BRANCH), BF16),
        scratch_shapes=[pltpu.VMEM((8, 3 * BRANCH), F32), pltpu.VMEM((8, LORA_W), F32),
                        pltpu.VMEM((N_GROUPS, HEAD_DIM, GROUP_HEADS * HEAD_DIM), F32)]
                       + [pltpu.VMEM((tt, BRANCH), F32) for _ in range(7)],
        compiler_params=_cparams(("parallel", "arbitrary")),
        name="rwkv7",
    )(proj3, proj3, proj3, proj3, mu_r, mu_k, mu_v, mu_lo,
      row(w0), w2d.astype(BF16), row(a0), w2i.astype(BF16), w2g.astype(BF16),
      row(k_k), row(k_a), row(r_k), row(lnx_g), row(lnx_b))


def _forget_kernel(x_ref, wf_ref, bf_ref, nat_ref, carry_ref):
    tt = x_ref.shape[1]

    @pl.when(pl.program_id(1) == 0)
    def _():
        carry_ref[...] = jnp.zeros_like(carry_ref)

    f = jnp.dot(x_ref[0].astype(BF16), wf_ref[...], preferred_element_type=F32) + bf_ref[...]
    log_f = jnp.minimum(f, 0.0) - jnp.log1p(jnp.exp(-jnp.abs(f)))
    ri = lax.broadcasted_iota(jnp.int32, (tt, tt), 0)
    ci = lax.broadcasted_iota(jnp.int32, (tt, tt), 1)
    ltri = jnp.where(ri >= ci, 1.0, 0.0).astype(BF16)
    cum = carry_ref[0:1, :]
    for part in _split3(log_f):
        cum = cum + jnp.dot(ltri, part, preferred_element_type=F32)
    carry_ref[0:1, :] = cum[tt - 1:tt, :]
    nat_ref[0] = cum


def _forget_cumsum(x3, w_f, b_forget, tt=512):
    bsz, t_len, _ = x3.shape
    tt = min(tt, t_len)
    wf = jnp.zeros((D_MODEL, LANES), BF16).at[:, :N_HEADS].set(w_f.astype(BF16))
    bf = jnp.zeros((1, LANES), F32).at[0, :N_HEADS].set(b_forget)
    return pl.pallas_call(
        _forget_kernel,
        grid=(bsz, t_len // tt),
        in_specs=[pl.BlockSpec((1, tt, D_MODEL), lambda b, t: (b, t, 0)),
                  pl.BlockSpec((D_MODEL, LANES), lambda b, t: (0, 0)),
                  pl.BlockSpec((1, LANES), lambda b, t: (0, 0))],
        out_specs=pl.BlockSpec((1, tt, LANES), lambda b, t: (b, t, 0)),
        out_shape=jax.ShapeDtypeStruct((bsz, t_len, LANES), F32),
        scratch_shapes=[pltpu.VMEM((8, LANES), F32)],
        compiler_params=_cparams(("parallel", "arbitrary")),
        name="forget_cumsum",
    )(x3, wf, bf)


AUG_LANE = HEAD_DIM
LOG2E = math.log2(math.e)


def _bf16_pieces(x):
    p1 = x.astype(BF16).astype(F32)
    p2 = (x - p1).astype(BF16).astype(F32)
    p3 = (x - p1 - p2).astype(BF16).astype(F32)
    return p1, p2, p3


def _fox_prep_kernel(q_ref, k_ref, v_ref, cnat_ref, qa_ref, ka_ref, vt_ref):
    tt = q_ref.shape[1]
    q = q_ref[0].astype(F32) * (HEAD_DIM ** -0.5 * LOG2E)
    k = k_ref[0]
    v = v_ref[0]
    cnat = cnat_ref[0] * LOG2E
    lane = lax.broadcasted_iota(jnp.int32, (tt, LANES), 1)
    eye = jnp.where(lax.broadcasted_iota(jnp.int32, (HEAD_DIM, HEAD_DIM), 0)
                    == lax.broadcasted_iota(jnp.int32, (HEAD_DIM, HEAD_DIM), 1), 1.0, 0.0).astype(BF16)
    pad = jnp.zeros((tt, LANES - HEAD_DIM), F32)
    for h in range(N_HEADS):
        sl = slice(h * HEAD_DIM, (h + 1) * HEAD_DIM)
        c1, c2, c3 = _bf16_pieces(cnat[:, h:h + 1])
        ones_q = (lane >= AUG_LANE + 3) & (lane < AUG_LANE + 6)
        aug_q = jnp.where(lane == AUG_LANE, c1, jnp.where(lane == AUG_LANE + 1, c2,
                          jnp.where(lane == AUG_LANE + 2, c3, jnp.where(ones_q, 1.0, 0.0))))
        qa = jnp.where(lane < HEAD_DIM, jnp.concatenate([q[:, sl], pad], axis=1), aug_q)
        qa_ref[0, h] = qa.astype(BF16)
        ones_k = (lane >= AUG_LANE) & (lane < AUG_LANE + 3)
        aug_k = jnp.where(lane == AUG_LANE + 3, -c1, jnp.where(lane == AUG_LANE + 4, -c2,
                          jnp.where(lane == AUG_LANE + 5, -c3, jnp.where(ones_k, 1.0, 0.0))))
        ka = jnp.where(lane < HEAD_DIM, jnp.concatenate([k[:, sl].astype(F32), pad], axis=1), aug_k)
        ka_ref[0, h] = ka.astype(BF16)
        vt_ref[0, h, 0] = _dot_nt(eye, v[:, sl]).astype(BF16)


def _fox_prep(proj3, cum_nat, tt):
    bsz, t_len, _ = proj3.shape
    act = lambda col: pl.BlockSpec((1, tt, BRANCH), lambda b, t: (b, t, col // BRANCH))
    return pl.pallas_call(
        _fox_prep_kernel,
        grid=(bsz, t_len // tt),
        in_specs=[act(COL_FQ), act(COL_FK), act(COL_FV),
                  pl.BlockSpec((1, tt, LANES), lambda b, t: (b, t, 0))],
        out_specs=[pl.BlockSpec((1, N_HEADS, tt, LANES), lambda b, t: (b, 0, t, 0)),
                   pl.BlockSpec((1, N_HEADS, tt, LANES), lambda b, t: (b, 0, t, 0)),
                   pl.BlockSpec((1, N_HEADS, 1, HEAD_DIM, tt), lambda b, t: (b, 0, t, 0, 0))],
        out_shape=[jax.ShapeDtypeStruct((bsz, N_HEADS, t_len, LANES), BF16),
                   jax.ShapeDtypeStruct((bsz, N_HEADS, t_len, LANES), BF16),
                   jax.ShapeDtypeStruct((bsz, N_HEADS, t_len // tt, HEAD_DIM, tt), BF16)],
        compiler_params=_cparams(("parallel", "parallel")),
        name="fox_prep",
    )(proj3, proj3, proj3, cum_nat)


def _fox_kernel(qa_ref, ka_ref, vt_ref, o_ref, *, tq, tk):
    i = pl.program_id(2)
    n_chunks = tq // tk
    diag_mask = (lax.broadcasted_iota(jnp.int32, (tk, tk), 0)
                 <= lax.broadcasted_iota(jnp.int32, (tk, tk), 1))

    def step(j, carry, d):
        c_first = 0 if d is None else d
        out = []
        for hh in range(2):
            rows = pl.ds(pl.multiple_of(j * tk, tk), tk)
            s = lax.dot_general(ka_ref[0, hh, rows, :], qa_ref[0, hh, c_first * tk:, :],
                                (((1,), (1,)), ((), ())), preferred_element_type=F32)
            v_t = vt_ref[0, hh, j]
            chunks = list(carry[hh])
            for c in range(c_first, n_chunks):
                m, l, acc = chunks[c]
                s_c = s[:, (c - c_first) * tk:(c - c_first + 1) * tk]
                if c == d:
                    s_c = jnp.where(diag_mask, s_c, -1e30)
                m_new = jnp.maximum(m, jnp.max(s_c, axis=0, keepdims=True))
                alpha = jnp.exp2(m - m_new)
                p = jnp.exp2(s_c - m_new)
                l_new = alpha * l + jnp.sum(p, axis=0, keepdims=True)
                acc_new = alpha * acc + jnp.dot(v_t, p.astype(BF16), preferred_element_type=F32)
                chunks[c] = (m_new, l_new, acc_new)
            out.append(tuple(chunks))
        return tuple(out)

    one = (jnp.full((1, tk), -1e30, F32), jnp.zeros((1, tk), F32), jnp.zeros((HEAD_DIM, tk), F32))
    init = tuple(tuple(one for _ in range(n_chunks)) for _ in range(2))
    n_full = i * n_chunks
    carry = lax.fori_loop(0, n_full, lambda j, c: step(j, c, None), init)
    for d in range(n_chunks):
        carry = step(n_full + d, carry, d)
    out_t = jnp.concatenate(
        [jnp.concatenate([acc / l for (_, l, acc) in carry[hh]], axis=1) for hh in range(2)], axis=0)
    o_ref[0] = jnp.transpose(out_t).astype(o_ref.dtype)


def _fox_attention(proj3, cum_nat, tq=2048, tk=512):
    bsz, t_len, _ = proj3.shape
    tq = min(tq, t_len)
    tk = min(tk, tq)
    q_aug, k_aug, v_t = _fox_prep(proj3, cum_nat, tk)
    return pl.pallas_call(
        functools.partial(_fox_kernel, tq=tq, tk=tk),
        grid=(bsz, N_HEADS // 2, t_len // tq),
        in_specs=[pl.BlockSpec((1, 2, tq, LANES), lambda b, hp, i: (b, hp, i, 0)),
                  pl.BlockSpec((1, 2, t_len, LANES), lambda b, hp, i: (b, hp, 0, 0)),
                  pl.BlockSpec((1, 2, t_len // tk, HEAD_DIM, tk), lambda b, hp, i: (b, hp, 0, 0, 0))],
        out_specs=pl.BlockSpec((1, tq, LANES), lambda b, hp, i: (b, i, hp)),
        out_shape=jax.ShapeDtypeStruct((bsz, t_len, BRANCH), BF16),
        compiler_params=_cparams(("parallel", "parallel", "arbitrary")),
        name="fox_attention",
    )(q_aug, k_aug, v_t)


def _merge_kernel(oa_ref, cb_ref, cc_ref, ch_ref, ccp_ref, chp_ref, oc_ref,
                  g0_ref, g1_ref, g2_ref, x_ref, convw_ref, bg_ref,
                  wua_ref, wub_ref, wuc_ref, wo_ref, lg_ref, lb_ref, o_ref, *, alpha, tiles_per_seq):
    tm = x_ref.shape[0]
    first = (pl.program_id(0) % tiles_per_seq) == 0
    u = cc_ref[...].astype(F32) * ch_ref[...].astype(F32)
    u_prev = ccp_ref[...].astype(F32) * chp_ref[...].astype(F32)
    u_prev = jnp.where(first, 0.0, u_prev)
    row = lax.broadcasted_iota(jnp.int32, u.shape, 0)
    u1 = jnp.where(row == 0, u_prev[7:8, :], pltpu.roll(u, 1, 0))
    u2 = pltpu.roll(u, 2, 0)
    u2 = jnp.where(row == 0, u_prev[6:7, :], jnp.where(row == 1, u_prev[7:8, :], u2))
    cw = convw_ref[...]
    conv = cw[0:1, :] * u2 + cw[1:2, :] * u1 + cw[2:3, :] * u
    o_b = cb_ref[...].astype(F32) * conv

    bg = bg_ref[...]
    merged = _sigmoid(g0_ref[...].astype(F32) + bg[0:1, :]) * jnp.dot(
        oa_ref[...], wua_ref[...], preferred_element_type=F32)
    merged = merged + _sigmoid(g1_ref[...].astype(F32) + bg[1:2, :]) * _dot(o_b, wub_ref[...])
    merged = merged + _sigmoid(g2_ref[...].astype(F32) + bg[2:3, :]) * jnp.dot(
        oc_ref[...], wuc_ref[...], preferred_element_type=F32)
    mix = _dot(merged, wo_ref[...])
    o_ref[...] = _layer_norm(alpha * x_ref[...] + mix, lg_ref[...], lb_ref[...])


def _merge(o_a, proj, o_c, x2d, conv_w, b_gate, wua, wub, wuc, wo, ln_g, ln_b, alpha, t_len, tm=512):
    n = x2d.shape[0]
    tm = min(tm, t_len)
    tiles_per_seq = t_len // tm
    act = lambda col, width: pl.BlockSpec((tm, width), lambda i: (i, col // width))
    halo = lambda col: pl.BlockSpec(
        (8, BRANCH), lambda i: (jnp.maximum(i * (tm // 8) - 1, 0), col // BRANCH))
    full = lambda shape: pl.BlockSpec(shape, lambda i: (0, 0))
    row = lambda a: a.reshape(1, -1).astype(F32)
    return pl.pallas_call(
        functools.partial(_merge_kernel, alpha=alpha, tiles_per_seq=tiles_per_seq),
        grid=(n // tm,),
        in_specs=[act(0, BRANCH), act(COL_CB, BRANCH), act(COL_CC, BRANCH), act(COL_CH, BRANCH),
                  halo(COL_CC), halo(COL_CH), act(0, BRANCH),
                  act(COL_GATE, D_MODEL), act(COL_GATE + D_MODEL, D_MODEL),
                  act(COL_GATE + 2 * D_MODEL, D_MODEL),
                  act(0, D_MODEL), full((3, BRANCH)), full((3, D_MODEL)),
                  full((BRANCH, D_MODEL)), full((BRANCH, D_MODEL)), full((BRANCH, D_MODEL)),
                  full((D_MODEL, D_MODEL)), full((1, D_MODEL)), full((1, D_MODEL))],
        out_specs=pl.BlockSpec((tm, D_MODEL), lambda i: (i, 0)),
        out_shape=jax.ShapeDtypeStruct((n, D_MODEL), F32),
        compiler_params=_cparams(("parallel",)),
        name="merge",
    )(o_a, proj, proj, proj, proj, proj, o_c, proj, proj, proj, x2d,
      conv_w.astype(F32), b_gate.astype(F32),
      wua.astype(BF16), wub.astype(BF16), wuc.astype(BF16), wo.astype(BF16), row(ln_g), row(ln_b))


def _ffn_kernel(x_ref, w1_ref, w3_ref, w2_ref, lg_ref, lb_ref, o_ref, xb_ref, acc_ref, *, alpha):
    j = pl.program_id(1)

    @pl.when(j == 0)
    def _():
        xb_ref[...] = x_ref[...].astype(BF16)
        acc_ref[...] = jnp.zeros_like(acc_ref)

    xb = xb_ref[...]
    h1 = jnp.dot(xb, w1_ref[...], preferred_element_type=F32)
    h3 = jnp.dot(xb, w3_ref[...], preferred_element_type=F32)
    hidden = (h1 * _sigmoid(h1) * h3).astype(BF16)
    acc_ref[...] += jnp.dot(hidden, w2_ref[...], preferred_element_type=F32)

    @pl.when(j == pl.num_programs(1) - 1)
    def _():
        o_ref[...] = _layer_norm(alpha * x_ref[...] + acc_ref[...], lg_ref[...], lb_ref[...])


def _ffn_dense(x2d, w1, w3, w2, ln_g, ln_b, alpha, tm=1024, tf=256):
    n = x2d.shape[0]
    tm = min(tm, n)
    d_ff = w1.shape[1]
    row = lambda a: a.reshape(1, -1).astype(F32)
    return pl.pallas_call(
        functools.partial(_ffn_kernel, alpha=alpha),
        grid=(n // tm, d_ff // tf),
        in_specs=[pl.BlockSpec((tm, D_MODEL), lambda i, j: (i, 0)),
                  pl.BlockSpec((D_MODEL, tf), lambda i, j: (0, j)),
                  pl.BlockSpec((D_MODEL, tf), lambda i, j: (0, j)),
                  pl.BlockSpec((tf, D_MODEL), lambda i, j: (j, 0)),
                  pl.BlockSpec((1, D_MODEL), lambda i, j: (0, 0)),
                  pl.BlockSpec((1, D_MODEL), lambda i, j: (0, 0))],
        out_specs=pl.BlockSpec((tm, D_MODEL), lambda i, j: (i, 0)),
        out_shape=jax.ShapeDtypeStruct((n, D_MODEL), F32),
        scratch_shapes=[pltpu.VMEM((tm, D_MODEL), BF16), pltpu.VMEM((tm, D_MODEL), F32)],
        compiler_params=_cparams(("parallel", "arbitrary")),
        name="ffn_dense",
    )(x2d, w1.astype(BF16), w3.astype(BF16), w2.astype(BF16), row(ln_g), row(ln_b))


def _router_kernel(x_ref, w_ref, b_ref, meta_ref, gate_ref, total_ref, count_ref):
    @pl.when(pl.program_id(0) == 0)
    def _():
        count_ref[...] = jnp.zeros_like(count_ref)

    x = x_ref[...]
    tm = x.shape[0]
    logits = b_ref[...] + jnp.zeros((tm, LANES), F32)
    w_parts = (w_ref[0], w_ref[1], w_ref[2])
    x_parts = _split3(x)
    for xi in range(3):
        for wi in range(3 - xi):
            logits = logits + jnp.dot(x_parts[xi], w_parts[wi], preferred_element_type=F32)
    lane = lax.broadcasted_iota(jnp.int32, (tm, LANES), 1)
    neg = -1e30
    logits = jnp.where(lane < N_EXPERTS, logits, neg)
    m1 = jnp.max(logits, axis=1, keepdims=True)
    i1 = jnp.min(jnp.where(logits == m1, lane, LANES), axis=1, keepdims=True)
    rest = jnp.where(lane == i1, neg, logits)
    m2 = jnp.max(rest, axis=1, keepdims=True)
    i2 = jnp.min(jnp.where(rest == m2, lane, LANES), axis=1, keepdims=True)
    e2 = jnp.exp(m2 - m1)
    g1 = 1.0 / (1.0 + e2)
    g2 = e2 / (1.0 + e2)
    chosen = jnp.where((lane == i1) | (lane == i2), 1.0, 0.0)
    lstrict = jnp.where(lax.broadcasted_iota(jnp.int32, (tm, tm), 0)
                        > lax.broadcasted_iota(jnp.int32, (tm, tm), 1), 1.0, 0.0).astype(BF16)
    before = count_ref[0:1, :] + jnp.dot(lstrict, chosen.astype(BF16), preferred_element_type=F32)
    count_ref[0:1, :] = before[tm - 1:tm, :] + chosen[tm - 1:tm, :]
    total_ref[...] = jnp.broadcast_to(count_ref[0:1, :], total_ref.shape)
    rank1 = jnp.sum(jnp.where(lane == i1, before, 0.0), axis=1, keepdims=True)
    rank2 = jnp.sum(jnp.where(lane == i2, before, 0.0), axis=1, keepdims=True)
    meta = jnp.where(lane == 0, i1.astype(F32), jnp.where(lane == 1, i2.astype(F32),
                     jnp.where(lane == 2, rank1, jnp.where(lane == 3, rank2, 0.0))))
    meta_ref[...] = meta.astype(jnp.int32)
    gate_ref[...] = jnp.where(lane == 0, g1, jnp.where(lane == 1, g2, 0.0))


def _router(x2d, router_w, router_b, tm=512):
    n = x2d.shape[0]
    tm = min(tm, n)
    w_pad = jnp.zeros((D_MODEL, LANES), F32).at[:, :N_EXPERTS].set(router_w)
    w_parts = jnp.stack(_split3(w_pad))
    b_pad = jnp.zeros((1, LANES), F32).at[0, :N_EXPERTS].set(router_b)
    return pl.pallas_call(
        _router_kernel,
        grid=(n // tm,),
        in_specs=[pl.BlockSpec((tm, D_MODEL), lambda i: (i, 0)),
                  pl.BlockSpec((3, D_MODEL, LANES), lambda i: (0, 0, 0)),
                  pl.BlockSpec((1, LANES), lambda i: (0, 0))],
        out_specs=[pl.BlockSpec((tm, LANES), lambda i: (i, 0)),
                   pl.BlockSpec((tm, LANES), lambda i: (i, 0)),
                   pl.BlockSpec((8, LANES), lambda i: (0, 0))],
        out_shape=[jax.ShapeDtypeStruct((n, LANES), jnp.int32),
                   jax.ShapeDtypeStruct((n, LANES), F32),
                   jax.ShapeDtypeStruct((8, LANES), F32)],
        scratch_shapes=[pltpu.VMEM((8, LANES), F32)],
        compiler_params=_cparams(("arbitrary",)),
        name="router",
    )(x2d, w_parts, b_pad)


def _row_copies(src_ref, dst_ref, sem, src_row, dst_row):
    return pltpu.make_async_copy(src_ref.at[pl.ds(src_row, 1)], dst_ref.at[pl.ds(dst_row, 1)], sem)


def _dispatch_kernel(pos0_ref, pos1_ref, x_ref, zero_ref, xs_ref, sem):
    del zero_ref
    tm = x_ref.shape[0]

    def copies(r):
        return (_row_copies(x_ref, xs_ref, sem, r, pos0_ref[r]),
                _row_copies(x_ref, xs_ref, sem, r, pos1_ref[r]))

    def start(r, carry):
        for cp in copies(r):
            cp.start()
        return carry

    def wait(r, carry):
        for cp in copies(r):
            cp.wait()
        return carry

    lax.fori_loop(0, tm, start, 0)
    lax.fori_loop(0, tm, wait, 0)


def _dispatch(x2d, pos0, pos1, n_sorted, tm=512):
    n = x2d.shape[0]
    tm = min(tm, n)
    smem = lambda: pl.BlockSpec((tm,), lambda i: (i,), memory_space=pltpu.SMEM)
    return pl.pallas_call(
        _dispatch_kernel,
        grid=(n // tm,),
        in_specs=[smem(), smem(), pl.BlockSpec((tm, D_MODEL), lambda i: (i, 0)),
                  pl.BlockSpec(memory_space=pl.ANY)],
        out_specs=pl.BlockSpec(memory_space=pl.ANY),
        out_shape=jax.ShapeDtypeStruct((n_sorted, D_MODEL), F32),
        scratch_shapes=[pltpu.SemaphoreType.DMA(())],
        input_output_aliases={3: 0},
        compiler_params=_cparams(("arbitrary",)),
        name="moe_dispatch",
    )(pos0, pos1, x2d, jnp.zeros((n_sorted, D_MODEL), F32))


def _experts_kernel(te_ref, used_ref, x_ref, w1_ref, w3_ref, w2_ref, o_ref, xb_ref, acc_ref):
    i = pl.program_id(0)
    j = pl.program_id(1)
    last = pl.num_programs(1) - 1
    live = i < used_ref[0]

    @pl.when(live & (j == 0))
    def _():
        xb_ref[...] = x_ref[...].astype(BF16)
        acc_ref[...] = jnp.zeros_like(acc_ref)

    @pl.when(live)
    def _():
        xb = xb_ref[...]
        h1 = jnp.dot(xb, w1_ref[0], preferred_element_type=F32)
        h3 = jnp.dot(xb, w3_ref[0], preferred_element_type=F32)
        hidden = (h1 * _sigmoid(h1) * h3).astype(BF16)
        acc_ref[...] += jnp.dot(hidden, w2_ref[0], preferred_element_type=F32)

    @pl.when(live & (j == last))
    def _():
        o_ref[...] = acc_ref[...]

    @pl.when(jnp.logical_not(live) & (j == last))
    def _():
        o_ref[...] = jnp.zeros_like(o_ref)


def _experts(xs, tile_expert, n_used, w1, w3, w2, tm, tf=512):
    n_sorted = xs.shape[0]
    d_ff = w1.shape[2]
    n_f = d_ff // tf
    def jj(i, j, used):
        return jnp.where(i < used[0], j, n_f - 1)
    grid_spec = pltpu.PrefetchScalarGridSpec(
        num_scalar_prefetch=2,
        grid=(n_sorted // tm, n_f),
        in_specs=[pl.BlockSpec((tm, D_MODEL), lambda i, j, te, used: (jnp.minimum(i, used[0] - 1), 0)),
                  pl.BlockSpec((1, D_MODEL, tf), lambda i, j, te, used: (te[i], 0, jj(i, j, used))),
                  pl.BlockSpec((1, D_MODEL, tf), lambda i, j, te, used: (te[i], 0, jj(i, j, used))),
                  pl.BlockSpec((1, tf, D_MODEL), lambda i, j, te, used: (te[i], jj(i, j, used), 0))],
        out_specs=pl.BlockSpec((tm, D_MODEL), lambda i, j, te, used: (i, 0)),
        scratch_shapes=[pltpu.VMEM((tm, D_MODEL), BF16), pltpu.VMEM((tm, D_MODEL), F32)])
    return pl.pallas_call(
        _experts_kernel,
        grid_spec=grid_spec,
        out_shape=jax.ShapeDtypeStruct((n_sorted, D_MODEL), F32),
        compiler_params=_cparams(("arbitrary", "arbitrary")),
        name="moe_experts",
    )(tile_expert, n_used, xs, w1.astype(BF16), w3.astype(BF16), w2.astype(BF16))


def _combine_kernel(pos0_ref, pos1_ref, x_ref, gate_ref, lg_ref, lb_ref, ys_ref, o_ref, y0_ref, y1_ref, sem,
                    *, alpha):
    tm = x_ref.shape[0]

    def copies(r):
        return (_row_copies(ys_ref, y0_ref, sem, pos0_ref[r], r),
                _row_copies(ys_ref, y1_ref, sem, pos1_ref[r], r))

    def start(r, carry):
        for cp in copies(r):
            cp.start()
        return carry

    def wait(r, carry):
        for cp in copies(r):
            cp.wait()
        return carry

    lax.fori_loop(0, tm, start, 0)
    lax.fori_loop(0, tm, wait, 0)
    gates = gate_ref[...]
    ffn = gates[:, 0:1] * y0_ref[...] + gates[:, 1:2] * y1_ref[...]
    o_ref[...] = _layer_norm(alpha * x_ref[...] + ffn, lg_ref[...], lb_ref[...])


def _combine(x2d, ys, pos0, pos1, gates, ln_g, ln_b, alpha, tm=256):
    n = x2d.shape[0]
    tm = min(tm, n)
    smem = lambda: pl.BlockSpec((tm,), lambda i: (i,), memory_space=pltpu.SMEM)
    row = lambda a: a.reshape(1, -1).astype(F32)
    return pl.pallas_call(
        functools.partial(_combine_kernel, alpha=alpha),
        grid=(n // tm,),
        in_specs=[smem(), smem(), pl.BlockSpec((tm, D_MODEL), lambda i: (i, 0)),
                  pl.BlockSpec((tm, LANES), lambda i: (i, 0)),
                  pl.BlockSpec((1, D_MODEL), lambda i: (0, 0)), pl.BlockSpec((1, D_MODEL), lambda i: (0, 0)),
                  pl.BlockSpec(memory_space=pl.ANY)],
        out_specs=pl.BlockSpec((tm, D_MODEL), lambda i: (i, 0)),
        out_shape=jax.ShapeDtypeStruct((n, D_MODEL), F32),
        scratch_shapes=[pltpu.VMEM((tm, D_MODEL), F32), pltpu.VMEM((tm, D_MODEL), F32),
                        pltpu.SemaphoreType.DMA(())],
        compiler_params=_cparams(("arbitrary",)),
        name="moe_combine",
    )(pos0, pos1, x2d, gates, row(ln_g), row(ln_b), ys)


def _ffn_moe(x2d, router_w, router_b, w1, w3, w2, ln_g, ln_b, alpha, tm=1024):
    n = x2d.shape[0]
    tm = min(tm, n)
    meta, gates, totals = _router(x2d, router_w, router_b)
    counts = totals[0, :N_EXPERTS].astype(jnp.int32)
    padded = ((counts + tm - 1) // tm) * tm
    ends = jnp.cumsum(padded)
    starts = ends - padded
    pos0 = starts[meta[:, 0]] + meta[:, 2]
    pos1 = starts[meta[:, 1]] + meta[:, 3]
    n_sorted = 2 * n + N_EXPERTS * tm
    tile_start = jnp.arange(n_sorted // tm, dtype=jnp.int32) * tm
    tile_expert = jnp.minimum(jnp.searchsorted(ends, tile_start, side="right"), N_EXPERTS - 1).astype(jnp.int32)
    n_used = (ends[-1:] // tm).astype(jnp.int32)
    xs = _dispatch(x2d, pos0, pos1, n_sorted)
    ys = _experts(xs, tile_expert, n_used, w1, w3, w2, tm)
    return _combine(x2d, ys, pos0, pos1, gates, ln_g, ln_b, alpha)


def _arrange_w_in(w_in_l):
    rwkv_cols = 3 * BRANCH + LORA_W
    conv0 = rwkv_cols
    fox0 = conv0 + 3 * BRANCH
    f0 = fox0 + 3 * BRANCH
    gate0 = f0 + N_HEADS
    zeros = jnp.zeros((D_MODEL, COL_CB - COL_LORA - LORA_W), w_in_l.dtype)
    w_all = jnp.concatenate([w_in_l[:, :rwkv_cols], zeros, w_in_l[:, conv0:f0], w_in_l[:, gate0:]], axis=1)
    return w_all.astype(BF16), w_in_l[:, f0:gate0]


def kernel(x, ln0_g, ln0_b, w_in, mu_shift, w0_decay, w2_decay, a0, w2_iclr, w2_gate, k_k, k_a, r_k, lnx_g, lnx_b, conv_w, b_forget, b_gate, w_up_rwkv, w_up_conv, w_up_attn, w_out, ln1_g, ln1_b, ln2_g, ln2_b, ffn_w1, ffn_w3, ffn_w2, router_w, router_b, moe_w1, moe_w3, moe_w2):
    bsz, t_len, _ = x.shape
    n = bsz * t_len
    depth = w_in.shape[0]
    alpha = (2 * depth) ** 0.25
    xs = _ln0(x.reshape(n, D_MODEL), ln0_g, ln0_b)
    for l in range(depth):
        w_all, w_f = _arrange_w_in(w_in[l])
        proj = _proj(xs, w_all)
        proj3 = proj.reshape(bsz, t_len, PROJ_COLS)
        o_a = _rwkv(proj3, mu_shift[l], w0_decay[l], w2_decay[l], a0[l], w2_iclr[l], w2_gate[l],
                    k_k[l], k_a[l], r_k[l].reshape(-1), lnx_g[l], lnx_b[l])
        cum_nat = _forget_cumsum(xs.reshape(bsz, t_len, D_MODEL), w_f, b_forget[l])
        o_c = _fox_attention(proj3, cum_nat)
        xs = _merge(o_a.reshape(n, BRANCH), proj, o_c.reshape(n, BRANCH), xs, conv_w[l], b_gate[l],
                    w_up_rwkv[l], w_up_conv[l], w_up_attn[l], w_out[l], ln1_g[l], ln1_b[l], alpha, t_len)
        i = l // 2
        if l % 2 == 0:
            xs = _ffn_dense(xs, ffn_w1[i], ffn_w3[i], ffn_w2[i], ln2_g[l], ln2_b[l], alpha)
        else:
            xs = _ffn_moe(xs, router_w[i], router_b[i], moe_w1[i], moe_w3[i], moe_w2[i],
                          ln2_g[l], ln2_b[l], alpha)
    return xs.reshape(bsz, t_len, D_MODEL)
```

```python
import functools
import math

import jax
import jax.numpy as jnp
from jax import lax
from jax.experimental import pallas as pl
from jax.experimental.pallas import tpu as pltpu

F32 = jnp.float32
BF16 = jnp.bfloat16

D_MODEL = 1024
BRANCH = 512
HEAD_DIM = 64
N_HEADS = BRANCH // HEAD_DIM
LORA_W = 256
N_EXPERTS = 8
LANES = 128
RWKV_LN_EPS = 64e-5
LN_EPS = 1e-5
DECAY_SCALE = math.exp(-0.5)

COL_R, COL_K, COL_V, COL_LORA = 0, 512, 1024, 1536
COL_CB, COL_CC, COL_CH = 2048, 2560, 3072
COL_FQ, COL_FK, COL_FV = 3584, 4096, 4608
COL_GATE = 5120
PROJ_COLS = 8192

RWKV_CHUNK = 64
GROUP_HEADS = 4
N_GROUPS = N_HEADS // GROUP_HEADS
CHUNKS_PER_ITER = 8
VMEM_LIMIT = 56 * 1024 * 1024


def _cparams(sem):
    return pltpu.CompilerParams(dimension_semantics=sem, vmem_limit_bytes=VMEM_LIMIT)


def _dot(a, b):
    return jnp.dot(a.astype(BF16), b.astype(BF16), preferred_element_type=F32)


def _dot_nt(a, b):
    return lax.dot_general(a.astype(BF16), b.astype(BF16), (((1,), (1,)), ((), ())),
                           preferred_element_type=F32)


def _split2(x):
    hi = x.astype(BF16)
    lo = (x - hi.astype(F32)).astype(BF16)
    return hi, lo


def _split3(x):
    x1 = x.astype(BF16)
    r1 = x - x1.astype(F32)
    x2 = r1.astype(BF16)
    x3 = (r1 - x2.astype(F32)).astype(BF16)
    return x1, x2, x3


def _layer_norm(x, g, b):
    mu = jnp.mean(x, axis=-1, keepdims=True)
    xc = x - mu
    var = jnp.mean(xc * xc, axis=-1, keepdims=True)
    return xc * lax.rsqrt(var + LN_EPS) * g + b


def _sigmoid(x):
    return 1.0 / (1.0 + jnp.exp(-x))


def _ln0_kernel(x_ref, g_ref, b_ref, o_ref):
    o_ref[...] = _layer_norm(x_ref[...], g_ref[...], b_ref[...])


def _ln0(x2d, g, b, tm=1024):
    n = x2d.shape[0]
    tm = min(tm, n)
    return pl.pallas_call(
        _ln0_kernel,
        grid=(n // tm,),
        in_specs=[pl.BlockSpec((tm, D_MODEL), lambda i: (i, 0)),
                  pl.BlockSpec((1, D_MODEL), lambda i: (0, 0)),
                  pl.BlockSpec((1, D_MODEL), lambda i: (0, 0))],
        out_specs=pl.BlockSpec((tm, D_MODEL), lambda i: (i, 0)),
        out_shape=jax.ShapeDtypeStruct((n, D_MODEL), F32),
        compiler_params=_cparams(("parallel",)),
        name="ln0",
    )(x2d, g.reshape(1, -1), b.reshape(1, -1))


def _proj_kernel(x_ref, w_ref, o_ref, xb_ref):
    @pl.when(pl.program_id(1) == 0)
    def _():
        xb_ref[...] = x_ref[...].astype(BF16)

    o_ref[...] = jnp.dot(xb_ref[...], w_ref[...], preferred_element_type=F32).astype(o_ref.dtype)


def _proj(x2d, w, tm=1024, tn=1024):
    n = x2d.shape[0]
    cols = w.shape[1]
    tm = min(tm, n)
    return pl.pallas_call(
        _proj_kernel,
        grid=(n // tm, cols // tn),
        in_specs=[pl.BlockSpec((tm, D_MODEL), lambda i, j: (i, 0)),
                  pl.BlockSpec((D_MODEL, tn), lambda i, j: (0, j))],
        out_specs=pl.BlockSpec((tm, tn), lambda i, j: (i, j)),
        out_shape=jax.ShapeDtypeStruct((n, cols), BF16),
        scratch_shapes=[pltpu.VMEM((tm, D_MODEL), BF16)],
        compiler_params=_cparams(("parallel", "arbitrary")),
        name="in_proj",
    )(x2d, w)


def _shifted(x, prev_row):
    rolled = pltpu.roll(x, 1, 0)
    row = lax.broadcasted_iota(jnp.int32, x.shape, 0)
    return jnp.where(row == 0, prev_row, rolled)


def _rwkv_kernel(r_ref, k_ref, v_ref, lo_ref, mur_ref, muk_ref, muv_ref, mulo_ref,
                 w0_ref, w2d_ref, a0_ref, w2i_ref, w2g_ref, kk_ref, ka_ref, rk_ref,
                 lng_ref, lnb_ref, o_ref,
                 prev_ref, prevlo_ref, h_ref, r_s, lw_s, k_s, v_s, a_s, b_s, y_s):
    tt = r_ref.shape[1]
    c = RWKV_CHUNK

    @pl.when(pl.program_id(1) == 0)
    def _():
        prev_ref[...] = jnp.zeros_like(prev_ref)
        prevlo_ref[...] = jnp.zeros_like(prevlo_ref)
        h_ref[...] = jnp.zeros_like(h_ref)

    def lerp(ref, mu_ref, p_ref, idx, width):
        x = ref[0].astype(F32)
        prev_row = p_ref[0:1, idx * BRANCH: idx * BRANCH + width]
        out = x + mu_ref[...] * (_shifted(x, prev_row) - x)
        p_ref[0:1, idx * BRANCH: idx * BRANCH + width] = x[tt - 1:tt, :]
        return out

    r = lerp(r_ref, mur_ref, prev_ref, 0, BRANCH)
    k = lerp(k_ref, muk_ref, prev_ref, 1, BRANCH)
    v = lerp(v_ref, muv_ref, prev_ref, 2, BRANCH)
    lo = lerp(lo_ref, mulo_ref, prevlo_ref, 0, LORA_W)

    w_raw = w0_ref[...] + _dot(jnp.tanh(lo[:, 0:64]), w2d_ref[...])
    lw = -DECAY_SCALE * _sigmoid(w_raw)
    a = _sigmoid(a0_ref[...] + _dot(lo[:, 64:128], w2i_ref[...]))
    g = _dot(_sigmoid(lo[:, 128:256]), w2g_ref[...])

    rowh = lax.broadcasted_iota(jnp.int32, (BRANCH, BRANCH), 0) // HEAD_DIM
    colh = lax.broadcasted_iota(jnp.int32, (BRANCH, BRANCH), 1) // HEAD_DIM
    head_ones = jnp.where(rowh == colh, 1.0, 0.0).astype(BF16)

    def head_sum(x):
        hi, lo_ = _split2(x)
        return (jnp.dot(hi, head_ones, preferred_element_type=F32)
                + jnp.dot(lo_, head_ones, preferred_element_type=F32))

    kk = k * kk_ref[...]
    kk_norm = jnp.sqrt(head_sum(kk * kk))
    kkn = kk / jnp.maximum(kk_norm, 1e-12)
    k2 = k * (1.0 + (a - 1.0) * ka_ref[...])
    bonus = head_sum(r * k2 * rk_ref[...]) * v

    r_s[...] = r
    lw_s[...] = lw
    k_s[...] = k2
    v_s[...] = v
    a_s[...] = -kkn
    b_s[...] = kkn * a

    ltri = jnp.where(lax.broadcasted_iota(jnp.int32, (c, c), 0)
                     >= lax.broadcasted_iota(jnp.int32, (c, c), 1), 1.0, 0.0).astype(BF16)

    gw = GROUP_HEADS * HEAD_DIM
    t_w = lax.broadcasted_iota(jnp.int32, (c, gw), 0)
    s_w = lax.broadcasted_iota(jnp.int32, (c, gw), 1) % HEAD_DIM
    strict_w = t_w > s_w
    incl_w = t_w >= s_w
    eye_w = jnp.where(t_w == s_w, 1.0, 0.0)
    eye_wb = eye_w.astype(BF16)
    bd_mask = (lax.broadcasted_iota(jnp.int32, (gw, gw), 0) // HEAD_DIM
               == lax.broadcasted_iota(jnp.int32, (gw, gw), 1) // HEAD_DIM)

    def bd(x):
        xb = x.astype(BF16)
        return jnp.where(bd_mask, jnp.concatenate([xb] * GROUP_HEADS, axis=0), jnp.zeros((), BF16))

    def mm(a, b_bd):
        return jnp.dot(a.astype(BF16), b_bd, preferred_element_type=F32)

    def mm_nt(a, b_bd):
        return lax.dot_general(a.astype(BF16), b_bd, (((1,), (1,)), ((), ())), preferred_element_type=F32)

    def chunk_static(rows, g):
        sl = slice(g * gw, (g + 1) * gw)
        lw_c = lw_s[rows, sl]
        lw_hi, lw_lo = _split2(lw_c)
        cum = (jnp.dot(ltri, lw_hi, preferred_element_type=F32)
               + jnp.dot(ltri, lw_lo, preferred_element_type=F32))
        cum_last = cum[c - 1:c, :]
        p_inv = jnp.exp(-cum)
        p_tail = jnp.exp(cum_last - cum)
        b_c, k_c = b_s[rows, sl], k_s[rows, sl]
        ar = jnp.concatenate([a_s[rows, sl] * jnp.exp(cum - lw_c), r_s[rows, sl] * jnp.exp(cum)],
                             axis=0).astype(BF16)
        yield
        aab = mm_nt(ar, bd(b_c * p_inv))
        aak = mm_nt(ar, bd(k_c * p_inv))
        b_t = mm_nt(eye_wb, bd(b_c * p_tail)).astype(BF16)
        k_t = mm_nt(eye_wb, bd(k_c * p_tail)).astype(BF16)
        pl_hi, pl_lo = _split2(jnp.broadcast_to(jnp.exp(cum_last), (c, gw)))
        decay_w = mm_nt(eye_wb, bd(pl_hi)) + mm_nt(eye_wb, bd(pl_lo))
        yield
        n_w = jnp.where(strict_w, aab[0:c], 0.0)
        a_ark = jnp.concatenate([jnp.where(strict_w, aak[0:c], 0.0),
                                 jnp.where(incl_w, aak[c:2 * c], 0.0)], axis=0).astype(BF16)
        a_rb = jnp.where(incl_w, aab[c:2 * c], 0.0).astype(BF16)
        tinv = eye_w + n_w
        npow = n_w
        for _ in range(5):
            npow = mm(npow, bd(npow))
            yield
            tinv = tinv + mm(npow, bd(tinv))
            yield
        return ar, a_ark, a_rb, tinv.astype(BF16), b_t, k_t, decay_w, bd(v_s[rows, sl])

    def chunk_state(rows, g, st):
        ar, a_ark, a_rb, tinv, b_t, k_t, decay_w, v_bd = st
        h0 = h_ref[g]
        arh = mm(ar, bd(h0))
        akv = mm(a_ark, v_bd)
        yield
        u_bd = bd(mm(tinv, bd(arh[0:c] + akv[0:c])))
        yield
        y_s[rows, g * gw:(g + 1) * gw] = arh[c:2 * c] + akv[c:2 * c] + mm(a_rb, u_bd)
        h_ref[g] = decay_w * h0 + mm(b_t, u_bd) + mm(k_t, v_bd)

    def in_lockstep(gens):
        results = [None] * len(gens)
        live = list(range(len(gens)))
        while live:
            for idx in list(live):
                try:
                    next(gens[idx])
                except StopIteration as stop:
                    results[idx] = stop.value
                    live.remove(idx)
        return results

    def chunk_body(ic, carry):
        items = [(pl.ds(pl.multiple_of((ic * CHUNKS_PER_ITER + j) * c, c), c), g)
                 for j in range(CHUNKS_PER_ITER) for g in range(N_GROUPS)]
        statics = in_lockstep([chunk_static(r, g) for r, g in items])
        for j in range(CHUNKS_PER_ITER):
            in_lockstep([chunk_state(*items[j * N_GROUPS + g], statics[j * N_GROUPS + g])
                         for g in range(N_GROUPS)])
        return carry

    lax.fori_loop(0, tt // (c * CHUNKS_PER_ITER), chunk_body, 0)

    y = y_s[...]
    mean = head_sum(y) * (1.0 / HEAD_DIM)
    yc = y - mean
    var = head_sum(yc * yc) * (1.0 / HEAD_DIM)
    yn = yc * lax.rsqrt(var + RWKV_LN_EPS) * lng_ref[...] + lnb_ref[...]
    o_ref[0] = ((yn + bonus) * g).astype(o_ref.dtype)


def _rwkv(proj3, mu_shift, w0, w2d, a0, w2i, w2g, k_k, k_a, r_k, lnx_g, lnx_b, tt=512):
    bsz, t_len, _ = proj3.shape
    tt = min(tt, t_len)
    row = lambda a: a.reshape(1, -1).astype(F32)
    mu_r, mu_k, mu_v = (row(mu_shift[i * BRANCH:(i + 1) * BRANCH]) for i in range(3))
    mu_lo = row(mu_shift[3 * BRANCH:])
    vec = lambda width: pl.BlockSpec((1, width), lambda b, t: (0, 0))
    mat = lambda shape: pl.BlockSpec(shape, lambda b, t: (0, 0))
    act = lambda col, width: pl.BlockSpec((1, tt, width), lambda b, t: (b, t, col // width))
    return pl.pallas_call(
        _rwkv_kernel,
        grid=(bsz, t_len // tt),
        in_specs=[act(COL_R, BRANCH), act(COL_K, BRANCH), act(COL_V, BRANCH), act(COL_LORA, LORA_W),
                  vec(BRANCH), vec(BRANCH), vec(BRANCH), vec(LORA_W),
                  vec(BRANCH), mat((64, BRANCH)), vec(BRANCH), mat((64, BRANCH)), mat((128, BRANCH)),
                  vec(BRANCH), vec(BRANCH), vec(BRANCH), vec(BRANCH), vec(BRANCH)],
        out_specs=pl.BlockSpec((1, tt, BRANCH), lambda b, t: (b, t, 0)),
        out_shape=jax.ShapeDtypeStruct((bsz, t_len, BRANCH), BF16),
        scratch_shapes=[pltpu.VMEM((8, 3 * BRANCH), F32), pltpu.VMEM((8, LORA_W), F32),
                        pltpu.VMEM((N_GROUPS, HEAD_DIM, GROUP_HEADS * HEAD_DIM), F32)]
                       + [pltpu.VMEM((tt, BRANCH), F32) for _ in range(7)],
        compiler_params=_cparams(("parallel", "arbitrary")),
        name="rwkv7",
    )(proj3, proj3, proj3, proj3, mu_r, mu_k, mu_v, mu_lo,
      row(w0), w2d.astype(BF16), row(a0), w2i.astype(BF16), w2g.astype(BF16),
      row(k_k), row(k_a), row(r_k), row(lnx_g), row(lnx_b))


def _forget_kernel(x_ref, wf_ref, bf_ref, nat_ref, carry_ref):
    tt = x_ref.shape[1]

    @pl.when(pl.program_id(1) == 0)
    def _():
        carry_ref[...] = jnp.zeros_like(carry_ref)

    f = jnp.dot(x_ref[0].astype(BF16), wf_ref[...], preferred_element_type=F32) + bf_ref[...]
    log_f = jnp.minimum(f, 0.0) - jnp.log1p(jnp.exp(-jnp.abs(f)))
    ri = lax.broadcasted_iota(jnp.int32, (tt, tt), 0)
    ci = lax.broadcasted_iota(jnp.int32, (tt, tt), 1)
    ltri = jnp.where(ri >= ci, 1.0, 0.0).astype(BF16)
    cum = carry_ref[0:1, :]
    for part in _split3(log_f):
        cum = cum + jnp.dot(ltri, part, preferred_element_type=F32)
    carry_ref[0:1, :] = cum[tt - 1:tt, :]
    nat_ref[0] = cum


def _forget_cumsum(x3, w_f, b_forget, tt=512):
    bsz, t_len, _ = x3.shape
    tt = min(tt, t_len)
    wf = jnp.zeros((D_MODEL, LANES), BF16).at[:, :N_HEADS].set(w_f.astype(BF16))
    bf = jnp.zeros((1, LANES), F32).at[0, :N_HEADS].set(b_forget)
    return pl.pallas_call(
        _forget_kernel,
        grid=(bsz, t_len // tt),
        in_specs=[pl.BlockSpec((1, tt, D_MODEL), lambda b, t: (b, t, 0)),
                  pl.BlockSpec((D_MODEL, LANES), lambda b, t: (0, 0)),
                  pl.BlockSpec((1, LANES), lambda b, t: (0, 0))],
        out_specs=pl.BlockSpec((1, tt, LANES), lambda b, t: (b, t, 0)),
        out_shape=jax.ShapeDtypeStruct((bsz, t_len, LANES), F32),
        scratch_shapes=[pltpu.VMEM((8, LANES), F32)],
        compiler_params=_cparams(("parallel", "arbitrary")),
        name="forget_cumsum",
    )(x3, wf, bf)


AUG_LANE = HEAD_DIM
VT_ROWS = HEAD_DIM + 16
KV_UNROLL = 2
LOG2E = math.log2(math.e)


def _bf16_pieces(x):
    p1 = x.astype(BF16).astype(F32)
    p2 = (x - p1).astype(BF16).astype(F32)
    p3 = (x - p1 - p2).astype(BF16).astype(F32)
    return p1, p2, p3


def _fox_prep_kernel(q_ref, k_ref, v_ref, cnat_ref, qa_ref, ka_ref, vt_ref):
    tt = q_ref.shape[1]
    q = q_ref[0].astype(F32) * (HEAD_DIM ** -0.5 * LOG2E)
    k = k_ref[0]
    v = v_ref[0]
    cnat = cnat_ref[0] * LOG2E
    lane = lax.broadcasted_iota(jnp.int32, (tt, LANES), 1)
    eye = jnp.where(lax.broadcasted_iota(jnp.int32, (HEAD_DIM, HEAD_DIM), 0)
                    == lax.broadcasted_iota(jnp.int32, (HEAD_DIM, HEAD_DIM), 1), 1.0, 0.0).astype(BF16)
    pad = jnp.zeros((tt, LANES - HEAD_DIM), F32)
    ones_rows = jnp.where(lax.broadcasted_iota(jnp.int32, (VT_ROWS - HEAD_DIM, tt), 0) == 0, 1.0, 0.0)
    for h in range(N_HEADS):
        sl = slice(h * HEAD_DIM, (h + 1) * HEAD_DIM)
        c1, c2, c3 = _bf16_pieces(cnat[:, h:h + 1])
        ones_q = (lane >= AUG_LANE + 3) & (lane < AUG_LANE + 6)
        aug_q = jnp.where(lane == AUG_LANE, c1, jnp.where(lane == AUG_LANE + 1, c2,
                          jnp.where(lane == AUG_LANE + 2, c3, jnp.where(ones_q, 1.0, 0.0))))
        qa = jnp.where(lane < HEAD_DIM, jnp.concatenate([q[:, sl], pad], axis=1), aug_q)
        qa_ref[0, h] = qa.astype(BF16)
        ones_k = (lane >= AUG_LANE) & (lane < AUG_LANE + 3)
        aug_k = jnp.where(lane == AUG_LANE + 3, -c1, jnp.where(lane == AUG_LANE + 4, -c2,
                          jnp.where(lane == AUG_LANE + 5, -c3, jnp.where(ones_k, 1.0, 0.0))))
        ka = jnp.where(lane < HEAD_DIM, jnp.concatenate([k[:, sl].astype(F32), pad], axis=1), aug_k)
        ka_ref[0, h] = ka.astype(BF16)
        vt_ref[0, h, 0] = jnp.concatenate([_dot_nt(eye, v[:, sl]), ones_rows], axis=0).astype(BF16)


def _fox_prep(proj3, cum_nat, tt):
    bsz, t_len, _ = proj3.shape
    act = lambda col: pl.BlockSpec((1, tt, BRANCH), lambda b, t: (b, t, col // BRANCH))
    return pl.pallas_call(
        _fox_prep_kernel,
        grid=(bsz, t_len // tt),
        in_specs=[act(COL_FQ), act(COL_FK), act(COL_FV),
                  pl.BlockSpec((1, tt, LANES), lambda b, t: (b, t, 0))],
        out_specs=[pl.BlockSpec((1, N_HEADS, tt, LANES), lambda b, t: (b, 0, t, 0)),
                   pl.BlockSpec((1, N_HEADS, tt, LANES), lambda b, t: (b, 0, t, 0)),
                   pl.BlockSpec((1, N_HEADS, 1, VT_ROWS, tt), lambda b, t: (b, 0, t, 0, 0))],
        out_shape=[jax.ShapeDtypeStruct((bsz, N_HEADS, t_len, LANES), BF16),
                   jax.ShapeDtypeStruct((bsz, N_HEADS, t_len, LANES), BF16),
                   jax.ShapeDtypeStruct((bsz, N_HEADS, t_len // tt, VT_ROWS, tt), BF16)],
        compiler_params=_cparams(("parallel", "parallel")),
        name="fox_prep",
    )(proj3, proj3, proj3, cum_nat)


def _fox_kernel(qa_ref, ka_ref, vt_ref, o_ref, m_ref, acc_ref, *, tq, tk):
    i = pl.program_id(2)
    n_chunks = tq // tk
    diag_mask = (lax.broadcasted_iota(jnp.int32, (tk, tk), 0)
                 <= lax.broadcasted_iota(jnp.int32, (tk, tk), 1))

    m_ref[...] = jnp.full(m_ref.shape, -1e30, F32)
    acc_ref[...] = jnp.zeros(acc_ref.shape, F32)

    def scores(j, hh, c):
        rows = pl.ds(pl.multiple_of(j * tk, tk), tk)
        return lax.dot_general(ka_ref[0, hh, rows, :], qa_ref[0, hh, c * tk:(c + 1) * tk, :],
                               (((1,), (1,)), ((), ())), preferred_element_type=F32)

    def update(j, hh, c, s, masked):
        if masked:
            s = jnp.where(diag_mask, s, -1e30)
        m = m_ref[hh, c, 0:1, :]
        m_new = jnp.maximum(m, jnp.max(s, axis=0, keepdims=True))
        alpha = jnp.exp2(m - m_new)
        p = jnp.exp2((s - m_new).astype(BF16))
        m_ref[hh, c, 0:1, :] = m_new
        acc_ref[hh, c] = alpha * acc_ref[hh, c] + jnp.dot(vt_ref[0, hh, j], p, preferred_element_type=F32)

    def step(j, d):
        units = [(hh, c) for c in range(0 if d is None else d, n_chunks) for hh in range(2)]
        s_next = scores(j, *units[0])
        for k, (hh, c) in enumerate(units):
            s_cur = s_next
            if k + 1 < len(units):
                s_next = scores(j, *units[k + 1])
            update(j, hh, c, s_cur, c == d)

    n_full = i * n_chunks

    def body(j2, carry):
        for u in range(KV_UNROLL):
            step(j2 * KV_UNROLL + u, None)
        return carry

    lax.fori_loop(0, n_full // KV_UNROLL, body, 0)
    for d in range(n_chunks):
        step(n_full + d, d)
    out_t = jnp.concatenate(
        [jnp.concatenate([acc_ref[hh, c, 0:HEAD_DIM, :] / acc_ref[hh, c, HEAD_DIM:HEAD_DIM + 1, :]
                          for c in range(n_chunks)], axis=1) for hh in range(2)], axis=0)
    o_ref[0] = jnp.transpose(out_t).astype(o_ref.dtype)


def _fox_attention(proj3, cum_nat, tq=2048, tk=512):
    bsz, t_len, _ = proj3.shape
    tq = min(tq, t_len)
    tk = min(tk, tq)
    q_aug, k_aug, v_t = _fox_prep(proj3, cum_nat, tk)
    return pl.pallas_call(
        functools.partial(_fox_kernel, tq=tq, tk=tk),
        grid=(bsz, N_HEADS // 2, t_len // tq),
        in_specs=[pl.BlockSpec((1, 2, tq, LANES), lambda b, hp, i: (b, hp, i, 0)),
                  pl.BlockSpec((1, 2, t_len, LANES), lambda b, hp, i: (b, hp, 0, 0)),
                  pl.BlockSpec((1, 2, t_len // tk, VT_ROWS, tk), lambda b, hp, i: (b, hp, 0, 0, 0))],
        out_specs=pl.BlockSpec((1, tq, LANES), lambda b, hp, i: (b, i, hp)),
        out_shape=jax.ShapeDtypeStruct((bsz, t_len, BRANCH), BF16),
        scratch_shapes=[pltpu.VMEM((2, tq // tk, 8, tk), F32), pltpu.VMEM((2, tq // tk, VT_ROWS, tk), F32)],
        compiler_params=_cparams(("parallel", "parallel", "arbitrary")),
        name="fox_attention",
    )(q_aug, k_aug, v_t)


def _merge_kernel(oa_ref, cb_ref, cc_ref, ch_ref, ccp_ref, chp_ref, oc_ref,
                  g0_ref, g1_ref, g2_ref, x_ref, convw_ref, bg_ref,
                  wua_ref, wub_ref, wuc_ref, wo_ref, lg_ref, lb_ref, o_ref, *, alpha, tiles_per_seq):
    tm = x_ref.shape[0]
    first = (pl.program_id(0) % tiles_per_seq) == 0
    u = cc_ref[...].astype(F32) * ch_ref[...].astype(F32)
    u_prev = ccp_ref[...].astype(F32) * chp_ref[...].astype(F32)
    u_prev = jnp.where(first, 0.0, u_prev)
    row = lax.broadcasted_iota(jnp.int32, u.shape, 0)
    u1 = jnp.where(row == 0, u_prev[7:8, :], pltpu.roll(u, 1, 0))
    u2 = pltpu.roll(u, 2, 0)
    u2 = jnp.where(row == 0, u_prev[6:7, :], jnp.where(row == 1, u_prev[7:8, :], u2))
    cw = convw_ref[...]
    conv = cw[0:1, :] * u2 + cw[1:2, :] * u1 + cw[2:3, :] * u
    o_b = cb_ref[...].astype(F32) * conv

    bg = bg_ref[...]
    merged = _sigmoid(g0_ref[...].astype(F32) + bg[0:1, :]) * jnp.dot(
        oa_ref[...], wua_ref[...], preferred_element_type=F32)
    merged = merged + _sigmoid(g1_ref[...].astype(F32) + bg[1:2, :]) * _dot(o_b, wub_ref[...])
    merged = merged + _sigmoid(g2_ref[...].astype(F32) + bg[2:3, :]) * jnp.dot(
        oc_ref[...], wuc_ref[...], preferred_element_type=F32)
    mix = _dot(merged, wo_ref[...])
    o_ref[...] = _layer_norm(alpha * x_ref[...] + mix, lg_ref[...], lb_ref[...])


def _merge(o_a, proj, o_c, x2d, conv_w, b_gate, wua, wub, wuc, wo, ln_g, ln_b, alpha, t_len, tm=512):
    n = x2d.shape[0]
    tm = min(tm, t_len)
    tiles_per_seq = t_len // tm
    act = lambda col, width: pl.BlockSpec((tm, width), lambda i: (i, col // width))
    halo = lambda col: pl.BlockSpec(
        (8, BRANCH), lambda i: (jnp.maximum(i * (tm // 8) - 1, 0), col // BRANCH))
    full = lambda shape: pl.BlockSpec(shape, lambda i: (0, 0))
    row = lambda a: a.reshape(1, -1).astype(F32)
    return pl.pallas_call(
        functools.partial(_merge_kernel, alpha=alpha, tiles_per_seq=tiles_per_seq),
        grid=(n // tm,),
        in_specs=[act(0, BRANCH), act(COL_CB, BRANCH), act(COL_CC, BRANCH), act(COL_CH, BRANCH),
                  halo(COL_CC), halo(COL_CH), act(0, BRANCH),
                  act(COL_GATE, D_MODEL), act(COL_GATE + D_MODEL, D_MODEL),
                  act(COL_GATE + 2 * D_MODEL, D_MODEL),
                  act(0, D_MODEL), full((3, BRANCH)), full((3, D_MODEL)),
                  full((BRANCH, D_MODEL)), full((BRANCH, D_MODEL)), full((BRANCH, D_MODEL)),
                  full((D_MODEL, D_MODEL)), full((1, D_MODEL)), full((1, D_MODEL))],
        out_specs=pl.BlockSpec((tm, D_MODEL), lambda i: (i, 0)),
        out_shape=jax.ShapeDtypeStruct((n, D_MODEL), F32),
        compiler_params=_cparams(("parallel",)),
        name="merge",
    )(o_a, proj, proj, proj, proj, proj, o_c, proj, proj, proj, x2d,
      conv_w.astype(F32), b_gate.astype(F32),
      wua.astype(BF16), wub.astype(BF16), wuc.astype(BF16), wo.astype(BF16), row(ln_g), row(ln_b))


def _ffn_kernel(x_ref, w1_ref, w3_ref, w2_ref, lg_ref, lb_ref, o_ref, xb_ref, acc_ref, *, alpha):
    j = pl.program_id(1)

    @pl.when(j == 0)
    def _():
        xb_ref[...] = x_ref[...].astype(BF16)
        acc_ref[...] = jnp.zeros_like(acc_ref)

    xb = xb_ref[...]
    h1 = jnp.dot(xb, w1_ref[...], preferred_element_type=F32)
    h3 = jnp.dot(xb, w3_ref[...], preferred_element_type=F32)
    hidden = (h1 * _sigmoid(h1) * h3).astype(BF16)
    acc_ref[...] += jnp.dot(hidden, w2_ref[...], preferred_element_type=F32)

    @pl.when(j == pl.num_programs(1) - 1)
    def _():
        o_ref[...] = _layer_norm(alpha * x_ref[...] + acc_ref[...], lg_ref[...], lb_ref[...])


def _ffn_dense(x2d, w1, w3, w2, ln_g, ln_b, alpha, tm=1024, tf=256):
    n = x2d.shape[0]
    tm = min(tm, n)
    d_ff = w1.shape[1]
    row = lambda a: a.reshape(1, -1).astype(F32)
    return pl.pallas_call(
        functools.partial(_ffn_kernel, alpha=alpha),
        grid=(n // tm, d_ff // tf),
        in_specs=[pl.BlockSpec((tm, D_MODEL), lambda i, j: (i, 0)),
                  pl.BlockSpec((D_MODEL, tf), lambda i, j: (0, j)),
                  pl.BlockSpec((D_MODEL, tf), lambda i, j: (0, j)),
                  pl.BlockSpec((tf, D_MODEL), lambda i, j: (j, 0)),
                  pl.BlockSpec((1, D_MODEL), lambda i, j: (0, 0)),
                  pl.BlockSpec((1, D_MODEL), lambda i, j: (0, 0))],
        out_specs=pl.BlockSpec((tm, D_MODEL), lambda i, j: (i, 0)),
        out_shape=jax.ShapeDtypeStruct((n, D_MODEL), F32),
        scratch_shapes=[pltpu.VMEM((tm, D_MODEL), BF16), pltpu.VMEM((tm, D_MODEL), F32)],
        compiler_params=_cparams(("parallel", "arbitrary")),
        name="ffn_dense",
    )(x2d, w1.astype(BF16), w3.astype(BF16), w2.astype(BF16), row(ln_g), row(ln_b))


def _router_kernel(x_ref, w_ref, b_ref, meta_ref, gate_ref, total_ref, count_ref):
    @pl.when(pl.program_id(0) == 0)
    def _():
        count_ref[...] = jnp.zeros_like(count_ref)

    x = x_ref[...]
    tm = x.shape[0]
    logits = b_ref[...] + jnp.zeros((tm, LANES), F32)
    w_parts = (w_ref[0], w_ref[1], w_ref[2])
    x_parts = _split3(x)
    for xi in range(3):
        for wi in range(3 - xi):
            logits = logits + jnp.dot(x_parts[xi], w_parts[wi], preferred_element_type=F32)
    lane = lax.broadcasted_iota(jnp.int32, (tm, LANES), 1)
    neg = -1e30
    logits = jnp.where(lane < N_EXPERTS, logits, neg)
    m1 = jnp.max(logits, axis=1, keepdims=True)
    i1 = jnp.min(jnp.where(logits == m1, lane, LANES), axis=1, keepdims=True)
    rest = jnp.where(lane == i1, neg, logits)
    m2 = jnp.max(rest, axis=1, keepdims=True)
    i2 = jnp.min(jnp.where(rest == m2, lane, LANES), axis=1, keepdims=True)
    e2 = jnp.exp(m2 - m1)
    g1 = 1.0 / (1.0 + e2)
    g2 = e2 / (1.0 + e2)
    chosen = jnp.where((lane == i1) | (lane == i2), 1.0, 0.0)
    lstrict = jnp.where(lax.broadcasted_iota(jnp.int32, (tm, tm), 0)
                        > lax.broadcasted_iota(jnp.int32, (tm, tm), 1), 1.0, 0.0).astype(BF16)
    before = count_ref[0:1, :] + jnp.dot(lstrict, chosen.astype(BF16), preferred_element_type=F32)
    count_ref[0:1, :] = before[tm - 1:tm, :] + chosen[tm - 1:tm, :]
    total_ref[...] = jnp.broadcast_to(count_ref[0:1, :], total_ref.shape)
    rank1 = jnp.sum(jnp.where(lane == i1, before, 0.0), axis=1, keepdims=True)
    rank2 = jnp.sum(jnp.where(lane == i2, before, 0.0), axis=1, keepdims=True)
    meta = jnp.where(lane == 0, i1.astype(F32), jnp.where(lane == 1, i2.astype(F32),
                     jnp.where(lane == 2, rank1, jnp.where(lane == 3, rank2, 0.0))))
    meta_ref[...] = meta.astype(jnp.int32)
    gate_ref[...] = jnp.where(lane == 0, g1, jnp.where(lane == 1, g2, 0.0))


def _router(x2d, router_w, router_b, tm=512):
    n = x2d.shape[0]
    tm = min(tm, n)
    w_pad = jnp.zeros((D_MODEL, LANES), F32).at[:, :N_EXPERTS].set(router_w)
    w_parts = jnp.stack(_split3(w_pad))
    b_pad = jnp.zeros((1, LANES), F32).at[0, :N_EXPERTS].set(router_b)
    return pl.pallas_call(
        _router_kernel,
        grid=(n // tm,),
        in_specs=[pl.BlockSpec((tm, D_MODEL), lambda i: (i, 0)),
                  pl.BlockSpec((3, D_MODEL, LANES), lambda i: (0, 0, 0)),
                  pl.BlockSpec((1, LANES), lambda i: (0, 0))],
        out_specs=[pl.BlockSpec((tm, LANES), lambda i: (i, 0)),
                   pl.BlockSpec((tm, LANES), lambda i: (i, 0)),
                   pl.BlockSpec((8, LANES), lambda i: (0, 0))],
        out_shape=[jax.ShapeDtypeStruct((n, LANES), jnp.int32),
                   jax.ShapeDtypeStruct((n, LANES), F32),
                   jax.ShapeDtypeStruct((8, LANES), F32)],
        scratch_shapes=[pltpu.VMEM((8, LANES), F32)],
        compiler_params=_cparams(("arbitrary",)),
        name="router",
    )(x2d, w_parts, b_pad)


DMA_UNROLL = 8


def _row_copies(src_ref, dst_ref, sem, src_row, dst_row):
    return pltpu.make_async_copy(src_ref.at[pl.ds(src_row, 1)], dst_ref.at[pl.ds(dst_row, 1)], sem)


def _dispatch_kernel(pos0_ref, pos1_ref, x_ref, zero_ref, xs_ref, sem):
    del zero_ref
    tm = x_ref.shape[0]

    def copies(r):
        return (_row_copies(x_ref, xs_ref, sem, r, pos0_ref[r]),
                _row_copies(x_ref, xs_ref, sem, r, pos1_ref[r]))

    def start(r, carry):
        for cp in copies(r):
            cp.start()
        return carry

    def wait(r, carry):
        for cp in copies(r):
            cp.wait()
        return carry

    lax.fori_loop(0, tm, start, 0, unroll=DMA_UNROLL)
    lax.fori_loop(0, tm, wait, 0, unroll=DMA_UNROLL)


def _dispatch(x2d, pos0, pos1, n_sorted, tm=512):
    n = x2d.shape[0]
    tm = min(tm, n)
    smem = lambda: pl.BlockSpec((tm,), lambda i: (i,), memory_space=pltpu.SMEM)
    return pl.pallas_call(
        _dispatch_kernel,
        grid=(n // tm,),
        in_specs=[smem(), smem(), pl.BlockSpec((tm, D_MODEL), lambda i: (i, 0)),
                  pl.BlockSpec(memory_space=pl.ANY)],
        out_specs=pl.BlockSpec(memory_space=pl.ANY),
        out_shape=jax.ShapeDtypeStruct((n_sorted, D_MODEL), F32),
        scratch_shapes=[pltpu.SemaphoreType.DMA(())],
        input_output_aliases={3: 0},
        compiler_params=_cparams(("arbitrary",)),
        name="moe_dispatch",
    )(pos0, pos1, x2d, jnp.zeros((n_sorted, D_MODEL), F32))


def _experts_kernel(te_ref, used_ref, x_ref, w1_ref, w3_ref, w2_ref, o_ref, xb_ref, acc_ref):
    i = pl.program_id(0)
    j = pl.program_id(1)
    last = pl.num_programs(1) - 1
    live = i < used_ref[0]

    @pl.when(live & (j == 0))
    def _():
        xb_ref[...] = x_ref[...].astype(BF16)
        acc_ref[...] = jnp.zeros_like(acc_ref)

    @pl.when(live)
    def _():
        xb = xb_ref[...]
        h1 = jnp.dot(xb, w1_ref[0], preferred_element_type=F32)
        h3 = jnp.dot(xb, w3_ref[0], preferred_element_type=F32)
        hidden = (h1 * _sigmoid(h1) * h3).astype(BF16)
        acc_ref[...] += jnp.dot(hidden, w2_ref[0], preferred_element_type=F32)

    @pl.when(live & (j == last))
    def _():
        o_ref[...] = acc_ref[...]

    @pl.when(jnp.logical_not(live) & (j == last))
    def _():
        o_ref[...] = jnp.zeros_like(o_ref)


def _experts(xs, tile_expert, n_used, w1, w3, w2, tm, tf=512):
    n_sorted = xs.shape[0]
    d_ff = w1.shape[2]
    n_f = d_ff // tf
    def jj(i, j, used):
        return jnp.where(i < used[0], j, n_f - 1)
    grid_spec = pltpu.PrefetchScalarGridSpec(
        num_scalar_prefetch=2,
        grid=(n_sorted // tm, n_f),
        in_specs=[pl.BlockSpec((tm, D_MODEL), lambda i, j, te, used: (jnp.minimum(i, used[0] - 1), 0)),
                  pl.BlockSpec((1, D_MODEL, tf), lambda i, j, te, used: (te[i], 0, jj(i, j, used))),
                  pl.BlockSpec((1, D_MODEL, tf), lambda i, j, te, used: (te[i], 0, jj(i, j, used))),
                  pl.BlockSpec((1, tf, D_MODEL), lambda i, j, te, used: (te[i], jj(i, j, used), 0))],
        out_specs=pl.BlockSpec((tm, D_MODEL), lambda i, j, te, used: (i, 0)),
        scratch_shapes=[pltpu.VMEM((tm, D_MODEL), BF16), pltpu.VMEM((tm, D_MODEL), F32)])
    return pl.pallas_call(
        _experts_kernel,
        grid_spec=grid_spec,
        out_shape=jax.ShapeDtypeStruct((n_sorted, D_MODEL), F32),
        compiler_params=_cparams(("arbitrary", "arbitrary")),
        name="moe_experts",
    )(tile_expert, n_used, xs, w1.astype(BF16), w3.astype(BF16), w2.astype(BF16))


def _combine_kernel(pos0_ref, pos1_ref, x_ref, gate_ref, lg_ref, lb_ref, ys_ref, o_ref, y0_ref, y1_ref, sem,
                    *, alpha):
    tm = x_ref.shape[0]

    def copies(r):
        return (_row_copies(ys_ref, y0_ref, sem, pos0_ref[r], r),
                _row_copies(ys_ref, y1_ref, sem, pos1_ref[r], r))

    def start(r, carry):
        for cp in copies(r):
            cp.start()
        return carry

    def wait(r, carry):
        for cp in copies(r):
            cp.wait()
        return carry

    lax.fori_loop(0, tm, start, 0, unroll=DMA_UNROLL)
    lax.fori_loop(0, tm, wait, 0, unroll=DMA_UNROLL)
    gates = gate_ref[...]
    ffn = gates[:, 0:1] * y0_ref[...] + gates[:, 1:2] * y1_ref[...]
    o_ref[...] = _layer_norm(alpha * x_ref[...] + ffn, lg_ref[...], lb_ref[...])


def _combine(x2d, ys, pos0, pos1, gates, ln_g, ln_b, alpha, tm=256):
    n = x2d.shape[0]
    tm = min(tm, n)
    smem = lambda: pl.BlockSpec((tm,), lambda i: (i,), memory_space=pltpu.SMEM)
    row = lambda a: a.reshape(1, -1).astype(F32)
    return pl.pallas_call(
        functools.partial(_combine_kernel, alpha=alpha),
        grid=(n // tm,),
        in_specs=[smem(), smem(), pl.BlockSpec((tm, D_MODEL), lambda i: (i, 0)),
                  pl.BlockSpec((tm, LANES), lambda i: (i, 0)),
                  pl.BlockSpec((1, D_MODEL), lambda i: (0, 0)), pl.BlockSpec((1, D_MODEL), lambda i: (0, 0)),
                  pl.BlockSpec(memory_space=pl.ANY)],
        out_specs=pl.BlockSpec((tm, D_MODEL), lambda i: (i, 0)),
        out_shape=jax.ShapeDtypeStruct((n, D_MODEL), F32),
        scratch_shapes=[pltpu.VMEM((tm, D_MODEL), F32), pltpu.VMEM((tm, D_MODEL), F32),
                        pltpu.SemaphoreType.DMA(())],
        compiler_params=_cparams(("arbitrary",)),
        name="moe_combine",
    )(pos0, pos1, x2d, gates, row(ln_g), row(ln_b), ys)


def _ffn_moe(x2d, router_w, router_b, w1, w3, w2, ln_g, ln_b, alpha, tm=1024):
    n = x2d.shape[0]
    tm = min(tm, n)
    meta, gates, totals = _router(x2d, router_w, router_b)
    counts = totals[0, :N_EXPERTS].astype(jnp.int32)
    padded = ((counts + tm - 1) // tm) * tm
    ends = jnp.cumsum(padded)
    starts = ends - padded
    pos0 = starts[meta[:, 0]] + meta[:, 2]
    pos1 = starts[meta[:, 1]] + meta[:, 3]
    n_sorted = 2 * n + N_EXPERTS * tm
    tile_start = jnp.arange(n_sorted // tm, dtype=jnp.int32) * tm
    tile_expert = jnp.minimum(jnp.sum((ends[None, :] <= tile_start[:, None]).astype(jnp.int32), axis=1),
                              N_EXPERTS - 1)
    n_used = (ends[-1:] // tm).astype(jnp.int32)
    xs = _dispatch(x2d, pos0, pos1, n_sorted)
    ys = _experts(xs, tile_expert, n_used, w1, w3, w2, tm)
    return _combine(x2d, ys, pos0, pos1, gates, ln_g, ln_b, alpha)


def _arrange_w_in(w_in_l):
    rwkv_cols = 3 * BRANCH + LORA_W
    conv0 = rwkv_cols
    fox0 = conv0 + 3 * BRANCH
    f0 = fox0 + 3 * BRANCH
    gate0 = f0 + N_HEADS
    zeros = jnp.zeros((D_MODEL, COL_CB - COL_LORA - LORA_W), w_in_l.dtype)
    w_all = jnp.concatenate([w_in_l[:, :rwkv_cols], zeros, w_in_l[:, conv0:f0], w_in_l[:, gate0:]], axis=1)
    return w_all.astype(BF16), w_in_l[:, f0:gate0]


def kernel(x, ln0_g, ln0_b, w_in, mu_shift, w0_decay, w2_decay, a0, w2_iclr, w2_gate, k_k, k_a, r_k, lnx_g, lnx_b, conv_w, b_forget, b_gate, w_up_rwkv, w_up_conv, w_up_attn, w_out, ln1_g, ln1_b, ln2_g, ln2_b, ffn_w1, ffn_w3, ffn_w2, router_w, router_b, moe_w1, moe_w3, moe_w2):
    bsz, t_len, _ = x.shape
    n = bsz * t_len
    depth = w_in.shape[0]
    alpha = (2 * depth) ** 0.25
    xs = _ln0(x.reshape(n, D_MODEL), ln0_g, ln0_b)
    for l in range(depth):
        w_all, w_f = _arrange_w_in(w_in[l])
        proj = _proj(xs, w_all)
        proj3 = proj.reshape(bsz, t_len, PROJ_COLS)
        o_a = _rwkv(proj3, mu_shift[l], w0_decay[l], w2_decay[l], a0[l], w2_iclr[l], w2_gate[l],
                    k_k[l], k_a[l], r_k[l].reshape(-1), lnx_g[l], lnx_b[l])
        cum_nat = _forget_cumsum(xs.reshape(bsz, t_len, D_MODEL), w_f, b_forget[l])
        o_c = _fox_attention(proj3, cum_nat)
        xs = _merge(o_a.reshape(n, BRANCH), proj, o_c.reshape(n, BRANCH), xs, conv_w[l], b_gate[l],
                    w_up_rwkv[l], w_up_conv[l], w_up_attn[l], w_out[l], ln1_g[l], ln1_b[l], alpha, t_len)
        i = l // 2
        if l % 2 == 0:
            xs = _ffn_dense(xs, ffn_w1[i], ffn_w3[i], ffn_w2[i], ln2_g[l], ln2_b[l], alpha)
        else:
            xs = _ffn_moe(xs, router_w[i], router_b[i], moe_w1[i], moe_w3[i], moe_w2[i],
                          ln2_g[l], ln2_b[l], alpha)
    return xs.reshape(bsz, t_len, D_MODEL)
```

```python
import functools
import math

import jax
import jax.numpy as jnp
from jax import lax
from jax.experimental import pallas as pl
from jax.experimental.pallas import tpu as pltpu

F32 = jnp.float32
BF16 = jnp.bfloat16

D_MODEL = 1024
BRANCH = 512
HEAD_DIM = 64
N_HEADS = BRANCH // HEAD_DIM
LORA_W = 256
N_EXPERTS = 8
LANES = 128
RWKV_LN_EPS = 64e-5
LN_EPS = 1e-5
DECAY_SCALE = math.exp(-0.5)

COL_R, COL_K, COL_V, COL_LORA = 0, 512, 1024, 1536
COL_CB, COL_CC, COL_CH = 2048, 2560, 3072
COL_FQ, COL_FK, COL_FV = 3584, 4096, 4608
COL_GATE = 5120
PROJ_COLS = 8192

RWKV_CHUNK = 64
GROUP_HEADS = 4
N_GROUPS = N_HEADS // GROUP_HEADS
CHUNKS_PER_ITER = 8
VMEM_LIMIT = 56 * 1024 * 1024


def _cparams(sem):
    return pltpu.CompilerParams(dimension_semantics=sem, vmem_limit_bytes=VMEM_LIMIT)


def _dot(a, b):
    return jnp.dot(a.astype(BF16), b.astype(BF16), preferred_element_type=F32)


def _dot_nt(a, b):
    return lax.dot_general(a.astype(BF16), b.astype(BF16), (((1,), (1,)), ((), ())),
                           preferred_element_type=F32)


def _split2(x):
    hi = x.astype(BF16)
    lo = (x - hi.astype(F32)).astype(BF16)
    return hi, lo


def _split3(x):
    x1 = x.astype(BF16)
    r1 = x - x1.astype(F32)
    x2 = r1.astype(BF16)
    x3 = (r1 - x2.astype(F32)).astype(BF16)
    return x1, x2, x3


def _layer_norm(x, g, b):
    mu = jnp.mean(x, axis=-1, keepdims=True)
    xc = x - mu
    var = jnp.mean(xc * xc, axis=-1, keepdims=True)
    return xc * lax.rsqrt(var + LN_EPS) * g + b


def _sigmoid(x):
    return 1.0 / (1.0 + jnp.exp(-x))


def _ln0_kernel(x_ref, g_ref, b_ref, o_ref):
    o_ref[...] = _layer_norm(x_ref[...], g_ref[...], b_ref[...])


def _ln0(x2d, g, b, tm=1024):
    n = x2d.shape[0]
    tm = min(tm, n)
    return pl.pallas_call(
        _ln0_kernel,
        grid=(n // tm,),
        in_specs=[pl.BlockSpec((tm, D_MODEL), lambda i: (i, 0)),
                  pl.BlockSpec((1, D_MODEL), lambda i: (0, 0)),
                  pl.BlockSpec((1, D_MODEL), lambda i: (0, 0))],
        out_specs=pl.BlockSpec((tm, D_MODEL), lambda i: (i, 0)),
        out_shape=jax.ShapeDtypeStruct((n, D_MODEL), F32),
        compiler_params=_cparams(("parallel",)),
        name="ln0",
    )(x2d, g.reshape(1, -1), b.reshape(1, -1))


def _proj_kernel(x_ref, w_ref, o_ref, xb_ref):
    @pl.when(pl.program_id(1) == 0)
    def _():
        xb_ref[...] = x_ref[...].astype(BF16)

    o_ref[...] = jnp.dot(xb_ref[...], w_ref[...], preferred_element_type=F32).astype(o_ref.dtype)


def _proj(x2d, w, tm=1024, tn=1024):
    n = x2d.shape[0]
    cols = w.shape[1]
    tm = min(tm, n)
    return pl.pallas_call(
        _proj_kernel,
        grid=(n // tm, cols // tn),
        in_specs=[pl.BlockSpec((tm, D_MODEL), lambda i, j: (i, 0)),
                  pl.BlockSpec((D_MODEL, tn), lambda i, j: (0, j))],
        out_specs=pl.BlockSpec((tm, tn), lambda i, j: (i, j)),
        out_shape=jax.ShapeDtypeStruct((n, cols), BF16),
        scratch_shapes=[pltpu.VMEM((tm, D_MODEL), BF16)],
        compiler_params=_cparams(("parallel", "arbitrary")),
        name="in_proj",
    )(x2d, w)


def _shifted(x, prev_row):
    rolled = pltpu.roll(x, 1, 0)
    row = lax.broadcasted_iota(jnp.int32, x.shape, 0)
    return jnp.where(row == 0, prev_row, rolled)


def _rwkv_kernel(r_ref, k_ref, v_ref, lo_ref, mur_ref, muk_ref, muv_ref, mulo_ref,
                 w0_ref, w2d_ref, a0_ref, w2i_ref, w2g_ref, kk_ref, ka_ref, rk_ref,
                 lng_ref, lnb_ref, o_ref,
                 prev_ref, prevlo_ref, h_ref, r_s, lw_s, k_s, v_s, a_s, b_s, y_s):
    tt = r_ref.shape[1]
    c = RWKV_CHUNK

    @pl.when(pl.program_id(1) == 0)
    def _():
        prev_ref[...] = jnp.zeros_like(prev_ref)
        prevlo_ref[...] = jnp.zeros_like(prevlo_ref)
        h_ref[...] = jnp.zeros_like(h_ref)

    def lerp(ref, mu_ref, p_ref, idx, width):
        x = ref[0].astype(F32)
        prev_row = p_ref[0:1, idx * BRANCH: idx * BRANCH + width]
        out = x + mu_ref[...] * (_shifted(x, prev_row) - x)
        p_ref[0:1, idx * BRANCH: idx * BRANCH + width] = x[tt - 1:tt, :]
        return out

    r = lerp(r_ref, mur_ref, prev_ref, 0, BRANCH)
    k = lerp(k_ref, muk_ref, prev_ref, 1, BRANCH)
    v = lerp(v_ref, muv_ref, prev_ref, 2, BRANCH)
    lo = lerp(lo_ref, mulo_ref, prevlo_ref, 0, LORA_W)

    w_raw = w0_ref[...] + _dot(jnp.tanh(lo[:, 0:64]), w2d_ref[...])
    lw = -DECAY_SCALE * _sigmoid(w_raw)
    a = _sigmoid(a0_ref[...] + _dot(lo[:, 64:128], w2i_ref[...]))
    g = _dot(_sigmoid(lo[:, 128:256]), w2g_ref[...])

    rowh = lax.broadcasted_iota(jnp.int32, (BRANCH, BRANCH), 0) // HEAD_DIM
    colh = lax.broadcasted_iota(jnp.int32, (BRANCH, BRANCH), 1) // HEAD_DIM
    head_ones = jnp.where(rowh == colh, 1.0, 0.0).astype(BF16)

    def head_sum(x):
        hi, lo_ = _split2(x)
        return (jnp.dot(hi, head_ones, preferred_element_type=F32)
                + jnp.dot(lo_, head_ones, preferred_element_type=F32))

    kk = k * kk_ref[...]
    kk_norm = jnp.sqrt(head_sum(kk * kk))
    kkn = kk / jnp.maximum(kk_norm, 1e-12)
    k2 = k * (1.0 + (a - 1.0) * ka_ref[...])
    bonus = head_sum(r * k2 * rk_ref[...]) * v

    r_s[...] = r
    lw_s[...] = lw
    k_s[...] = k2
    v_s[...] = v
    a_s[...] = -kkn
    b_s[...] = kkn * a

    ltri = jnp.where(lax.broadcasted_iota(jnp.int32, (c, c), 0)
                     >= lax.broadcasted_iota(jnp.int32, (c, c), 1), 1.0, 0.0).astype(BF16)

    gw = GROUP_HEADS * HEAD_DIM
    t_w = lax.broadcasted_iota(jnp.int32, (c, gw), 0)
    s_w = lax.broadcasted_iota(jnp.int32, (c, gw), 1) % HEAD_DIM
    strict_w = t_w > s_w
    incl_w = t_w >= s_w
    eye_w = jnp.where(t_w == s_w, 1.0, 0.0)
    eye_wb = eye_w.astype(BF16)
    bd_mask = (lax.broadcasted_iota(jnp.int32, (gw, gw), 0) // HEAD_DIM
               == lax.broadcasted_iota(jnp.int32, (gw, gw), 1) // HEAD_DIM)

    def bd(x):
        xb = x.astype(BF16)
        return jnp.where(bd_mask, jnp.concatenate([xb] * GROUP_HEADS, axis=0), jnp.zeros((), BF16))

    def mm(a, b_bd):
        return jnp.dot(a.astype(BF16), b_bd, preferred_element_type=F32)

    def mm_nt(a, b_bd):
        return lax.dot_general(a.astype(BF16), b_bd, (((1,), (1,)), ((), ())), preferred_element_type=F32)

    def chunk_static(rows, g):
        sl = slice(g * gw, (g + 1) * gw)
        lw_c = lw_s[rows, sl]
        lw_hi, lw_lo = _split2(lw_c)
        cum = (jnp.dot(ltri, lw_hi, preferred_element_type=F32)
               + jnp.dot(ltri, lw_lo, preferred_element_type=F32))
        cum_last = cum[c - 1:c, :]
        p_inv = jnp.exp(-cum)
        p_tail = jnp.exp(cum_last - cum)
        b_c, k_c = b_s[rows, sl], k_s[rows, sl]
        ar = jnp.concatenate([a_s[rows, sl] * jnp.exp(cum - lw_c), r_s[rows, sl] * jnp.exp(cum)],
                             axis=0).astype(BF16)
        yield
        aab = mm_nt(ar, bd(b_c * p_inv))
        aak = mm_nt(ar, bd(k_c * p_inv))
        b_t = mm_nt(eye_wb, bd(b_c * p_tail)).astype(BF16)
        k_t = mm_nt(eye_wb, bd(k_c * p_tail)).astype(BF16)
        pl_hi, pl_lo = _split2(jnp.broadcast_to(jnp.exp(cum_last), (c, gw)))
        decay_w = mm_nt(eye_wb, bd(pl_hi)) + mm_nt(eye_wb, bd(pl_lo))
        yield
        n_w = jnp.where(strict_w, aab[0:c], 0.0)
        a_ark = jnp.concatenate([jnp.where(strict_w, aak[0:c], 0.0),
                                 jnp.where(incl_w, aak[c:2 * c], 0.0)], axis=0).astype(BF16)
        a_rb = jnp.where(incl_w, aab[c:2 * c], 0.0).astype(BF16)
        tinv = eye_w + n_w
        npow = n_w
        for _ in range(5):
            npow = mm(npow, bd(npow))
            yield
            tinv = tinv + mm(npow, bd(tinv))
            yield
        return ar, a_ark, a_rb, tinv.astype(BF16), b_t, k_t, decay_w, bd(v_s[rows, sl])

    def chunk_state(rows, g, st):
        ar, a_ark, a_rb, tinv, b_t, k_t, decay_w, v_bd = st
        h0 = h_ref[g]
        arh = mm(ar, bd(h0))
        akv = mm(a_ark, v_bd)
        yield
        u_bd = bd(mm(tinv, bd(arh[0:c] + akv[0:c])))
        yield
        y_s[rows, g * gw:(g + 1) * gw] = arh[c:2 * c] + akv[c:2 * c] + mm(a_rb, u_bd)
        h_ref[g] = decay_w * h0 + mm(b_t, u_bd) + mm(k_t, v_bd)

    def in_lockstep(gens):
        results = [None] * len(gens)
        live = list(range(len(gens)))
        while live:
            for idx in list(live):
                try:
                    next(gens[idx])
                except StopIteration as stop:
                    results[idx] = stop.value
                    live.remove(idx)
        return results

    def chunk_body(ic, carry):
        items = [(pl.ds(pl.multiple_of((ic * CHUNKS_PER_ITER + j) * c, c), c), g)
                 for j in range(CHUNKS_PER_ITER) for g in range(N_GROUPS)]
        statics = in_lockstep([chunk_static(r, g) for r, g in items])
        for j in range(CHUNKS_PER_ITER):
            in_lockstep([chunk_state(*items[j * N_GROUPS + g], statics[j * N_GROUPS + g])
                         for g in range(N_GROUPS)])
        return carry

    lax.fori_loop(0, tt // (c * CHUNKS_PER_ITER), chunk_body, 0)

    y = y_s[...]
    mean = head_sum(y) * (1.0 / HEAD_DIM)
    yc = y - mean
    var = head_sum(yc * yc) * (1.0 / HEAD_DIM)
    yn = yc * lax.rsqrt(var + RWKV_LN_EPS) * lng_ref[...] + lnb_ref[...]
    o_ref[0] = ((yn + bonus) * g).astype(o_ref.dtype)


def _rwkv(proj3, mu_shift, w0, w2d, a0, w2i, w2g, k_k, k_a, r_k, lnx_g, lnx_b, tt=512):
    bsz, t_len, _ = proj3.shape
    tt = min(tt, t_len)
    row = lambda a: a.reshape(1, -1).astype(F32)
    mu_r, mu_k, mu_v = (row(mu_shift[i * BRANCH:(i + 1) * BRANCH]) for i in range(3))
    mu_lo = row(mu_shift[3 * BRANCH:])
    vec = lambda width: pl.BlockSpec((1, width), lambda b, t: (0, 0))
    mat = lambda shape: pl.BlockSpec(shape, lambda b, t: (0, 0))
    act = lambda col, width: pl.BlockSpec((1, tt, width), lambda b, t: (b, t, col // width))
    return pl.pallas_call(
        _rwkv_kernel,
        grid=(bsz, t_len // tt),
        in_specs=[act(COL_R, BRANCH), act(COL_K, BRANCH), act(COL_V, BRANCH), act(COL_LORA, LORA_W),
                  vec(BRANCH), vec(BRANCH), vec(BRANCH), vec(LORA_W),
                  vec(BRANCH), mat((64, BRANCH)), vec(BRANCH), mat((64, BRANCH)), mat((128, BRANCH)),
                  vec(BRANCH), vec(BRANCH), vec(BRANCH), vec(BRANCH), vec(BRANCH)],
        out_specs=pl.BlockSpec((1, tt, BRANCH), lambda b, t: (b, t, 0)),
        out_shape=jax.ShapeDtypeStruct((bsz, t_len, BRANCH), BF16),
        scratch_shapes=[pltpu.VMEM((8, 3 * BRANCH), F32), pltpu.VMEM((8, LORA_W), F32),
                        pltpu.VMEM((N_GROUPS, HEAD_DIM, GROUP_HEADS * HEAD_DIM), F32)]
                       + [pltpu.VMEM((tt, BRANCH), F32) for _ in range(7)],
        compiler_params=_cparams(("parallel", "arbitrary")),
        name="rwkv7",
    )(proj3, proj3, proj3, proj3, mu_r, mu_k, mu_v, mu_lo,
      row(w0), w2d.astype(BF16), row(a0), w2i.astype(BF16), w2g.astype(BF16),
      row(k_k), row(k_a), row(r_k), row(lnx_g), row(lnx_b))


def _forget_kernel(x_ref, wf_ref, bf_ref, nat_ref, carry_ref):
    tt = x_ref.shape[1]

    @pl.when(pl.program_id(1) == 0)
    def _():
        carry_ref[...] = jnp.zeros_like(carry_ref)

    f = jnp.dot(x_ref[0].astype(BF16), wf_ref[...], preferred_element_type=F32) + bf_ref[...]
    log_f = jnp.minimum(f, 0.0) - jnp.log1p(jnp.exp(-jnp.abs(f)))
    ri = lax.broadcasted_iota(jnp.int32, (tt, tt), 0)
    ci = lax.broadcasted_iota(jnp.int32, (tt, tt), 1)
    ltri = jnp.where(ri >= ci, 1.0, 0.0).astype(BF16)
    cum = carry_ref[0:1, :]
    for part in _split3(log_f):
        cum = cum + jnp.dot(ltri, part, preferred_element_type=F32)
    carry_ref[0:1, :] = cum[tt - 1:tt, :]
    nat_ref[0] = cum


def _forget_cumsum(x3, w_f, b_forget, tt=512):
    bsz, t_len, _ = x3.shape
    tt = min(tt, t_len)
    wf = jnp.zeros((D_MODEL, LANES), BF16).at[:, :N_HEADS].set(w_f.astype(BF16))
    bf = jnp.zeros((1, LANES), F32).at[0, :N_HEADS].set(b_forget)
    return pl.pallas_call(
        _forget_kernel,
        grid=(bsz, t_len // tt),
        in_specs=[pl.BlockSpec((1, tt, D_MODEL), lambda b, t: (b, t, 0)),
                  pl.BlockSpec((D_MODEL, LANES), lambda b, t: (0, 0)),
                  pl.BlockSpec((1, LANES), lambda b, t: (0, 0))],
        out_specs=pl.BlockSpec((1, tt, LANES), lambda b, t: (b, t, 0)),
        out_shape=jax.ShapeDtypeStruct((bsz, t_len, LANES), F32),
        scratch_shapes=[pltpu.VMEM((8, LANES), F32)],
        compiler_params=_cparams(("parallel", "arbitrary")),
        name="forget_cumsum",
    )(x3, wf, bf)


AUG_LANE = HEAD_DIM
VT_ROWS = HEAD_DIM + 16
KV_UNROLL = 2
SCORE_LOOKAHEAD = 2
LOG2E = math.log2(math.e)
UNDERFLOW_LOG2 = 152.0
NORM_SLACK = 1.01


def _bf16_pieces(x):
    p1 = x.astype(BF16).astype(F32)
    p2 = (x - p1).astype(BF16).astype(F32)
    p3 = (x - p1 - p2).astype(BF16).astype(F32)
    return p1, p2, p3


def _fox_prep_kernel(q_ref, k_ref, v_ref, cnat_ref, qa_ref, ka_ref, vt_ref, qn_ref, kn_ref):
    tt = q_ref.shape[1]
    q = q_ref[0].astype(F32) * (HEAD_DIM ** -0.5 * LOG2E)
    k = k_ref[0]
    v = v_ref[0]
    cnat = cnat_ref[0] * LOG2E
    lane = lax.broadcasted_iota(jnp.int32, (tt, LANES), 1)
    eye = jnp.where(lax.broadcasted_iota(jnp.int32, (HEAD_DIM, HEAD_DIM), 0)
                    == lax.broadcasted_iota(jnp.int32, (HEAD_DIM, HEAD_DIM), 1), 1.0, 0.0).astype(BF16)
    pad = jnp.zeros((tt, LANES - HEAD_DIM), F32)
    ones_rows = jnp.where(lax.broadcasted_iota(jnp.int32, (VT_ROWS - HEAD_DIM, tt), 0) == 0, 1.0, 0.0)

    def max_sq_norm(x):
        n2 = jnp.max(jnp.sum(x * x, axis=1, keepdims=True), axis=0, keepdims=True)
        return jnp.broadcast_to(n2, (1, LANES))

    q_norms, k_norms = [], []
    for h in range(N_HEADS):
        sl = slice(h * HEAD_DIM, (h + 1) * HEAD_DIM)
        c1, c2, c3 = _bf16_pieces(cnat[:, h:h + 1])
        ones_q = (lane >= AUG_LANE + 3) & (lane < AUG_LANE + 6)
        aug_q = jnp.where(lane == AUG_LANE, c1, jnp.where(lane == AUG_LANE + 1, c2,
                          jnp.where(lane == AUG_LANE + 2, c3, jnp.where(ones_q, 1.0, 0.0))))
        q_h = q[:, sl].astype(BF16).astype(F32)
        k_h = k[:, sl].astype(F32)
        qa_ref[0, h] = jnp.where(lane < HEAD_DIM, jnp.concatenate([q_h, pad], axis=1), aug_q).astype(BF16)
        ones_k = (lane >= AUG_LANE) & (lane < AUG_LANE + 3)
        aug_k = jnp.where(lane == AUG_LANE + 3, -c1, jnp.where(lane == AUG_LANE + 4, -c2,
                          jnp.where(lane == AUG_LANE + 5, -c3, jnp.where(ones_k, 1.0, 0.0))))
        ka_ref[0, h] = jnp.where(lane < HEAD_DIM, jnp.concatenate([k_h, pad], axis=1), aug_k).astype(BF16)
        vt_ref[0, h, 0] = jnp.concatenate([_dot_nt(eye, v[:, sl]), ones_rows], axis=0).astype(BF16)
        q_norms.append(max_sq_norm(q_h))
        k_norms.append(max_sq_norm(k_h))
    qn_ref[0, 0] = jnp.concatenate(q_norms, axis=0)
    kn_ref[0, 0] = jnp.concatenate(k_norms, axis=0)


def _fox_prep(proj3, cum_nat, tt):
    bsz, t_len, _ = proj3.shape
    n_t = t_len // tt
    act = lambda col: pl.BlockSpec((1, tt, BRANCH), lambda b, t: (b, t, col // BRANCH))
    norm_spec = pl.BlockSpec((1, 1, N_HEADS, LANES), lambda b, t: (b, t, 0, 0))
    return pl.pallas_call(
        _fox_prep_kernel,
        grid=(bsz, n_t),
        in_specs=[act(COL_FQ), act(COL_FK), act(COL_FV),
                  pl.BlockSpec((1, tt, LANES), lambda b, t: (b, t, 0))],
        out_specs=[pl.BlockSpec((1, N_HEADS, tt, LANES), lambda b, t: (b, 0, t, 0)),
                   pl.BlockSpec((1, N_HEADS, tt, LANES), lambda b, t: (b, 0, t, 0)),
                   pl.BlockSpec((1, N_HEADS, 1, VT_ROWS, tt), lambda b, t: (b, 0, t, 0, 0)),
                   norm_spec, norm_spec],
        out_shape=[jax.ShapeDtypeStruct((bsz, N_HEADS, t_len, LANES), BF16),
                   jax.ShapeDtypeStruct((bsz, N_HEADS, t_len, LANES), BF16),
                   jax.ShapeDtypeStruct((bsz, N_HEADS, n_t, VT_ROWS, tt), BF16),
                   jax.ShapeDtypeStruct((bsz, n_t, N_HEADS, LANES), F32),
                   jax.ShapeDtypeStruct((bsz, n_t, N_HEADS, LANES), F32)],
        compiler_params=_cparams(("parallel", "parallel")),
        name="fox_prep",
    )(proj3, proj3, proj3, cum_nat)


def _fox_kernel(first_ref, qa_ref, ka_ref, vt_ref, o_ref, m_ref, acc_ref, *, tq, tk):
    b, hp, i = pl.program_id(0), pl.program_id(1), pl.program_id(2)
    n_chunks = tq // tk
    diag_mask = (lax.broadcasted_iota(jnp.int32, (tk, tk), 0)
                 <= lax.broadcasted_iota(jnp.int32, (tk, tk), 1))

    m_ref[...] = jnp.full(m_ref.shape, -1e30, F32)
    acc_ref[...] = jnp.zeros(acc_ref.shape, F32)

    def scores(j, hh, c):
        rows = pl.ds(pl.multiple_of(j * tk, tk), tk)
        return lax.dot_general(ka_ref[0, hh, rows, :], qa_ref[0, hh, c * tk:(c + 1) * tk, :],
                               (((1,), (1,)), ((), ())), preferred_element_type=F32)

    def update(j, hh, c, s, masked):
        if masked:
            s = jnp.where(diag_mask, s, -1e30)
        m = m_ref[hh, c, 0:1, :]
        m_new = jnp.maximum(m, jnp.max(s, axis=0, keepdims=True))
        alpha = jnp.exp2(m - m_new)
        p = jnp.exp2((s - m_new).astype(BF16))
        m_ref[hh, c, 0:1, :] = m_new
        acc_ref[hh, c] = alpha * acc_ref[hh, c] + jnp.dot(vt_ref[0, hh, j], p, preferred_element_type=F32)

    def step(j, hh, d):
        units = list(range(0 if d is None else d, n_chunks))
        pending = [scores(j, hh, c) for c in units[:SCORE_LOOKAHEAD]]
        for k, c in enumerate(units):
            if k + SCORE_LOOKAHEAD < len(units):
                pending.append(scores(j, hh, units[k + SCORE_LOOKAHEAD]))
            update(j, hh, c, pending.pop(0), c == d)

    n_full = i * n_chunks
    for hh in range(2):
        first = first_ref[(b * N_HEADS + hp * 2 + hh) * pl.num_programs(2) + i]

        def body(j2, carry, hh=hh):
            for u in range(KV_UNROLL):
                step(j2 * KV_UNROLL + u, hh, None)
            return carry

        lax.fori_loop(first // KV_UNROLL, n_full // KV_UNROLL, body, 0)
        for d in range(n_chunks):
            step(n_full + d, hh, d)
    out_t = jnp.concatenate(
        [jnp.concatenate([acc_ref[hh, c, 0:HEAD_DIM, :] / acc_ref[hh, c, HEAD_DIM:HEAD_DIM + 1, :]
                          for c in range(n_chunks)], axis=1) for hh in range(2)], axis=0)
    o_ref[0] = jnp.transpose(out_t).astype(o_ref.dtype)


def _first_key_blocks(cum_nat, q_norm2, k_norm2, tq, tk):
    bound = jnp.sqrt(jnp.max(q_norm2[..., 0], axis=1) * jnp.max(k_norm2[..., 0], axis=1)) * NORM_SLACK
    cum2 = cum_nat[:, :, :N_HEADS] * LOG2E
    cq_start = cum2[:, ::tq, :]
    ck_end = cum2[:, tk - 1::tk, :]
    bias_ub = cq_start[:, :, None, :] - ck_end[:, None, :, :]
    n_q, n_k = cq_start.shape[1], ck_end.shape[1]
    below_diag = jnp.arange(n_k)[None, :] < (jnp.arange(n_q) * (tq // tk))[:, None]
    dead = (bias_ub < -(2.0 * bound[:, None, None, :] + UNDERFLOW_LOG2)) & below_diag[None, :, :, None]
    first = jnp.sum(dead.astype(jnp.int32), axis=2)
    return jnp.transpose(first, (0, 2, 1)).reshape(-1)


def _fox_attention(proj3, cum_nat, tq=2048, tk=512):
    bsz, t_len, _ = proj3.shape
    tq = min(tq, t_len)
    tk = min(tk, tq)
    q_aug, k_aug, v_t, q_norm2, k_norm2 = _fox_prep(proj3, cum_nat, tk)
    first = _first_key_blocks(cum_nat, q_norm2, k_norm2, tq, tk)
    grid_spec = pltpu.PrefetchScalarGridSpec(
        num_scalar_prefetch=1,
        grid=(bsz, N_HEADS // 2, t_len // tq),
        in_specs=[pl.BlockSpec((1, 2, tq, LANES), lambda b, hp, i, first: (b, hp, i, 0)),
                  pl.BlockSpec((1, 2, t_len, LANES), lambda b, hp, i, first: (b, hp, 0, 0)),
                  pl.BlockSpec((1, 2, t_len // tk, VT_ROWS, tk), lambda b, hp, i, first: (b, hp, 0, 0, 0))],
        out_specs=pl.BlockSpec((1, tq, LANES), lambda b, hp, i, first: (b, i, hp)),
        scratch_shapes=[pltpu.VMEM((2, tq // tk, 8, tk), F32), pltpu.VMEM((2, tq // tk, VT_ROWS, tk), F32)])
    return pl.pallas_call(
        functools.partial(_fox_kernel, tq=tq, tk=tk),
        grid_spec=grid_spec,
        out_shape=jax.ShapeDtypeStruct((bsz, t_len, BRANCH), BF16),
        compiler_params=_cparams(("parallel", "parallel", "arbitrary")),
        name="fox_attention",
    )(first, q_aug, k_aug, v_t)


def _merge_kernel(oa_ref, cb_ref, cc_ref, ch_ref, ccp_ref, chp_ref, oc_ref,
                  g0_ref, g1_ref, g2_ref, x_ref, convw_ref, bg_ref,
                  wua_ref, wub_ref, wuc_ref, wo_ref, lg_ref, lb_ref, o_ref, *, alpha, tiles_per_seq):
    tm = x_ref.shape[0]
    first = (pl.program_id(0) % tiles_per_seq) == 0
    u = cc_ref[...].astype(F32) * ch_ref[...].astype(F32)
    u_prev = ccp_ref[...].astype(F32) * chp_ref[...].astype(F32)
    u_prev = jnp.where(first, 0.0, u_prev)
    row = lax.broadcasted_iota(jnp.int32, u.shape, 0)
    u1 = jnp.where(row == 0, u_prev[7:8, :], pltpu.roll(u, 1, 0))
    u2 = pltpu.roll(u, 2, 0)
    u2 = jnp.where(row == 0, u_prev[6:7, :], jnp.where(row == 1, u_prev[7:8, :], u2))
    cw = convw_ref[...]
    conv = cw[0:1, :] * u2 + cw[1:2, :] * u1 + cw[2:3, :] * u
    o_b = cb_ref[...].astype(F32) * conv

    bg = bg_ref[...]
    merged = _sigmoid(g0_ref[...].astype(F32) + bg[0:1, :]) * jnp.dot(
        oa_ref[...], wua_ref[...], preferred_element_type=F32)
    merged = merged + _sigmoid(g1_ref[...].astype(F32) + bg[1:2, :]) * _dot(o_b, wub_ref[...])
    merged = merged + _sigmoid(g2_ref[...].astype(F32) + bg[2:3, :]) * jnp.dot(
        oc_ref[...], wuc_ref[...], preferred_element_type=F32)
    mix = _dot(merged, wo_ref[...])
    o_ref[...] = _layer_norm(alpha * x_ref[...] + mix, lg_ref[...], lb_ref[...])


def _merge(o_a, proj, o_c, x2d, conv_w, b_gate, wua, wub, wuc, wo, ln_g, ln_b, alpha, t_len, tm=512):
    n = x2d.shape[0]
    tm = min(tm, t_len)
    tiles_per_seq = t_len // tm
    act = lambda col, width: pl.BlockSpec((tm, width), lambda i: (i, col // width))
    halo = lambda col: pl.BlockSpec(
        (8, BRANCH), lambda i: (jnp.maximum(i * (tm // 8) - 1, 0), col // BRANCH))
    full = lambda shape: pl.BlockSpec(shape, lambda i: (0, 0))
    row = lambda a: a.reshape(1, -1).astype(F32)
    return pl.pallas_call(
        functools.partial(_merge_kernel, alpha=alpha, tiles_per_seq=tiles_per_seq),
        grid=(n // tm,),
        in_specs=[act(0, BRANCH), act(COL_CB, BRANCH), act(COL_CC, BRANCH), act(COL_CH, BRANCH),
                  halo(COL_CC), halo(COL_CH), act(0, BRANCH),
                  act(COL_GATE, D_MODEL), act(COL_GATE + D_MODEL, D_MODEL),
                  act(COL_GATE + 2 * D_MODEL, D_MODEL),
                  act(0, D_MODEL), full((3, BRANCH)), full((3, D_MODEL)),
                  full((BRANCH, D_MODEL)), full((BRANCH, D_MODEL)), full((BRANCH, D_MODEL)),
                  full((D_MODEL, D_MODEL)), full((1, D_MODEL)), full((1, D_MODEL))],
        out_specs=pl.BlockSpec((tm, D_MODEL), lambda i: (i, 0)),
        out_shape=jax.ShapeDtypeStruct((n, D_MODEL), F32),
        compiler_params=_cparams(("parallel",)),
        name="merge",
    )(o_a, proj, proj, proj, proj, proj, o_c, proj, proj, proj, x2d,
      conv_w.astype(F32), b_gate.astype(F32),
      wua.astype(BF16), wub.astype(BF16), wuc.astype(BF16), wo.astype(BF16), row(ln_g), row(ln_b))


def _ffn_kernel(x_ref, w1_ref, w3_ref, w2_ref, lg_ref, lb_ref, o_ref, xb_ref, acc_ref, *, alpha):
    j = pl.program_id(1)

    @pl.when(j == 0)
    def _():
        xb_ref[...] = x_ref[...].astype(BF16)
        acc_ref[...] = jnp.zeros_like(acc_ref)

    xb = xb_ref[...]
    h1 = jnp.dot(xb, w1_ref[...], preferred_element_type=F32)
    h3 = jnp.dot(xb, w3_ref[...], preferred_element_type=F32)
    hidden = (h1 * _sigmoid(h1) * h3).astype(BF16)
    acc_ref[...] += jnp.dot(hidden, w2_ref[...], preferred_element_type=F32)

    @pl.when(j == pl.num_programs(1) - 1)
    def _():
        o_ref[...] = _layer_norm(alpha * x_ref[...] + acc_ref[...], lg_ref[...], lb_ref[...])


def _ffn_dense(x2d, w1, w3, w2, ln_g, ln_b, alpha, tm=1024, tf=256):
    n = x2d.shape[0]
    tm = min(tm, n)
    d_ff = w1.shape[1]
    row = lambda a: a.reshape(1, -1).astype(F32)
    return pl.pallas_call(
        functools.partial(_ffn_kernel, alpha=alpha),
        grid=(n // tm, d_ff // tf),
        in_specs=[pl.BlockSpec((tm, D_MODEL), lambda i, j: (i, 0)),
                  pl.BlockSpec((D_MODEL, tf), lambda i, j: (0, j)),
                  pl.BlockSpec((D_MODEL, tf), lambda i, j: (0, j)),
                  pl.BlockSpec((tf, D_MODEL), lambda i, j: (j, 0)),
                  pl.BlockSpec((1, D_MODEL), lambda i, j: (0, 0)),
                  pl.BlockSpec((1, D_MODEL), lambda i, j: (0, 0))],
        out_specs=pl.BlockSpec((tm, D_MODEL), lambda i, j: (i, 0)),
        out_shape=jax.ShapeDtypeStruct((n, D_MODEL), F32),
        scratch_shapes=[pltpu.VMEM((tm, D_MODEL), BF16), pltpu.VMEM((tm, D_MODEL), F32)],
        compiler_params=_cparams(("parallel", "arbitrary")),
        name="ffn_dense",
    )(x2d, w1.astype(BF16), w3.astype(BF16), w2.astype(BF16), row(ln_g), row(ln_b))


def _router_kernel(x_ref, w_ref, b_ref, meta_ref, gate_ref, total_ref, count_ref):
    @pl.when(pl.program_id(0) == 0)
    def _():
        count_ref[...] = jnp.zeros_like(count_ref)

    x = x_ref[...]
    tm = x.shape[0]
    logits = b_ref[...] + jnp.zeros((tm, LANES), F32)
    w_parts = (w_ref[0], w_ref[1], w_ref[2])
    x_parts = _split3(x)
    for xi in range(2):
        for wi in range(2 - xi):
            logits = logits + jnp.dot(x_parts[xi], w_parts[wi], preferred_element_type=F32)
    lane = lax.broadcasted_iota(jnp.int32, (tm, LANES), 1)
    neg = -1e30
    logits = jnp.where(lane < N_EXPERTS, logits, neg)
    m1 = jnp.max(logits, axis=1, keepdims=True)
    i1 = jnp.min(jnp.where(logits == m1, lane, LANES), axis=1, keepdims=True)
    rest = jnp.where(lane == i1, neg, logits)
    m2 = jnp.max(rest, axis=1, keepdims=True)
    i2 = jnp.min(jnp.where(rest == m2, lane, LANES), axis=1, keepdims=True)
    e2 = jnp.exp(m2 - m1)
    g1 = 1.0 / (1.0 + e2)
    g2 = e2 / (1.0 + e2)
    chosen = jnp.where((lane == i1) | (lane == i2), 1.0, 0.0)
    lstrict = jnp.where(lax.broadcasted_iota(jnp.int32, (tm, tm), 0)
                        > lax.broadcasted_iota(jnp.int32, (tm, tm), 1), 1.0, 0.0).astype(BF16)
    before = count_ref[0:1, :] + jnp.dot(lstrict, chosen.astype(BF16), preferred_element_type=F32)
    count_ref[0:1, :] = before[tm - 1:tm, :] + chosen[tm - 1:tm, :]
    total_ref[...] = jnp.broadcast_to(count_ref[0:1, :], total_ref.shape)
    rank1 = jnp.sum(jnp.where(lane == i1, before, 0.0), axis=1, keepdims=True)
    rank2 = jnp.sum(jnp.where(lane == i2, before, 0.0), axis=1, keepdims=True)
    meta = jnp.where(lane == 0, i1.astype(F32), jnp.where(lane == 1, i2.astype(F32),
                     jnp.where(lane == 2, rank1, jnp.where(lane == 3, rank2, 0.0))))
    meta_ref[...] = meta.astype(jnp.int32)
    gate_ref[...] = jnp.where(lane == 0, g1, jnp.where(lane == 1, g2, 0.0))


def _router(x2d, router_w, router_b, tm=512):
    n = x2d.shape[0]
    tm = min(tm, n)
    w_pad = jnp.zeros((D_MODEL, LANES), F32).at[:, :N_EXPERTS].set(router_w)
    w_parts = jnp.stack(_split3(w_pad))
    b_pad = jnp.zeros((1, LANES), F32).at[0, :N_EXPERTS].set(router_b)
    return pl.pallas_call(
        _router_kernel,
        grid=(n // tm,),
        in_specs=[pl.BlockSpec((tm, D_MODEL), lambda i: (i, 0)),
                  pl.BlockSpec((3, D_MODEL, LANES), lambda i: (0, 0, 0)),
                  pl.BlockSpec((1, LANES), lambda i: (0, 0))],
        out_specs=[pl.BlockSpec((tm, LANES), lambda i: (i, 0)),
                   pl.BlockSpec((tm, LANES), lambda i: (i, 0)),
                   pl.BlockSpec((8, LANES), lambda i: (0, 0))],
        out_shape=[jax.ShapeDtypeStruct((n, LANES), jnp.int32),
                   jax.ShapeDtypeStruct((n, LANES), F32),
                   jax.ShapeDtypeStruct((8, LANES), F32)],
        scratch_shapes=[pltpu.VMEM((8, LANES), F32)],
        compiler_params=_cparams(("arbitrary",)),
        name="router",
    )(x2d, w_parts, b_pad)


DMA_UNROLL = 8


def _row_copies(src_ref, dst_ref, sem, src_row, dst_row):
    return pltpu.make_async_copy(src_ref.at[pl.ds(src_row, 1)], dst_ref.at[pl.ds(dst_row, 1)], sem)


def _dispatch_kernel(pos0_ref, pos1_ref, x_ref, zero_ref, xs_ref, sem):
    del zero_ref
    tm = x_ref.shape[0]

    def copies(r):
        return (_row_copies(x_ref, xs_ref, sem, r, pos0_ref[r]),
                _row_copies(x_ref, xs_ref, sem, r, pos1_ref[r]))

    def start(r, carry):
        for cp in copies(r):
            cp.start()
        return carry

    def wait(r, carry):
        for cp in copies(r):
            cp.wait()
        return carry

    lax.fori_loop(0, tm, start, 0, unroll=DMA_UNROLL)
    lax.fori_loop(0, tm, wait, 0, unroll=DMA_UNROLL)


def _dispatch(x2d, pos0, pos1, n_sorted, tm=512):
    n = x2d.shape[0]
    tm = min(tm, n)
    smem = lambda: pl.BlockSpec((tm,), lambda i: (i,), memory_space=pltpu.SMEM)
    return pl.pallas_call(
        _dispatch_kernel,
        grid=(n // tm,),
        in_specs=[smem(), smem(), pl.BlockSpec((tm, D_MODEL), lambda i: (i, 0)),
                  pl.BlockSpec(memory_space=pl.ANY)],
        out_specs=pl.BlockSpec(memory_space=pl.ANY),
        out_shape=jax.ShapeDtypeStruct((n_sorted, D_MODEL), F32),
        scratch_shapes=[pltpu.SemaphoreType.DMA(())],
        input_output_aliases={3: 0},
        compiler_params=_cparams(("arbitrary",)),
        name="moe_dispatch",
    )(pos0, pos1, x2d, jnp.zeros((n_sorted, D_MODEL), F32))


def _experts_kernel(te_ref, used_ref, x_ref, w1_ref, w3_ref, w2_ref, o_ref, xb_ref, acc_ref):
    i = pl.program_id(0)
    j = pl.program_id(1)
    last = pl.num_programs(1) - 1
    live = i < used_ref[0]

    @pl.when(live & (j == 0))
    def _():
        xb_ref[...] = x_ref[...].astype(BF16)
        acc_ref[...] = jnp.zeros_like(acc_ref)

    @pl.when(live)
    def _():
        xb = xb_ref[...]
        h1 = jnp.dot(xb, w1_ref[0].astype(BF16), preferred_element_type=F32)
        h3 = jnp.dot(xb, w3_ref[0].astype(BF16), preferred_element_type=F32)
        hidden = (h1 * _sigmoid(h1) * h3).astype(BF16)
        acc_ref[...] += jnp.dot(hidden, w2_ref[0].astype(BF16), preferred_element_type=F32)

    @pl.when(live & (j == last))
    def _():
        o_ref[...] = acc_ref[...]

    @pl.when(jnp.logical_not(live) & (j == last))
    def _():
        o_ref[...] = jnp.zeros_like(o_ref)


def _experts(xs, tile_expert, n_used, w1, w3, w2, tm, tf=512):
    n_sorted = xs.shape[0]
    d_ff = w1.shape[2]
    n_f = d_ff // tf
    def jj(i, j, used):
        return jnp.where(i < used[0], j, n_f - 1)
    grid_spec = pltpu.PrefetchScalarGridSpec(
        num_scalar_prefetch=2,
        grid=(n_sorted // tm, n_f),
        in_specs=[pl.BlockSpec((tm, D_MODEL), lambda i, j, te, used: (jnp.minimum(i, used[0] - 1), 0)),
                  pl.BlockSpec((1, D_MODEL, tf), lambda i, j, te, used: (te[i], 0, jj(i, j, used))),
                  pl.BlockSpec((1, D_MODEL, tf), lambda i, j, te, used: (te[i], 0, jj(i, j, used))),
                  pl.BlockSpec((1, tf, D_MODEL), lambda i, j, te, used: (te[i], jj(i, j, used), 0))],
        out_specs=pl.BlockSpec((tm, D_MODEL), lambda i, j, te, used: (i, 0)),
        scratch_shapes=[pltpu.VMEM((tm, D_MODEL), BF16), pltpu.VMEM((tm, D_MODEL), F32)])
    return pl.pallas_call(
        _experts_kernel,
        grid_spec=grid_spec,
        out_shape=jax.ShapeDtypeStruct((n_sorted, D_MODEL), F32),
        compiler_params=_cparams(("arbitrary", "arbitrary")),
        name="moe_experts",
    )(tile_expert, n_used, xs, w1, w3, w2)


def _combine_kernel(pos0_ref, pos1_ref, x_ref, gate_ref, lg_ref, lb_ref, ys_ref, o_ref, y0_ref, y1_ref, sem,
                    *, alpha):
    tm = x_ref.shape[0]

    def copies(r):
        return (_row_copies(ys_ref, y0_ref, sem, pos0_ref[r], r),
                _row_copies(ys_ref, y1_ref, sem, pos1_ref[r], r))

    def start(r, carry):
        for cp in copies(r):
            cp.start()
        return carry

    def wait(r, carry):
        for cp in copies(r):
            cp.wait()
        return carry

    lax.fori_loop(0, tm, start, 0, unroll=DMA_UNROLL)
    lax.fori_loop(0, tm, wait, 0, unroll=DMA_UNROLL)
    gates = gate_ref[...]
    ffn = gates[:, 0:1] * y0_ref[...] + gates[:, 1:2] * y1_ref[...]
    o_ref[...] = _layer_norm(alpha * x_ref[...] + ffn, lg_ref[...], lb_ref[...])


def _combine(x2d, ys, pos0, pos1, gates, ln_g, ln_b, alpha, tm=256):
    n = x2d.shape[0]
    tm = min(tm, n)
    smem = lambda: pl.BlockSpec((tm,), lambda i: (i,), memory_space=pltpu.SMEM)
    row = lambda a: a.reshape(1, -1).astype(F32)
    return pl.pallas_call(
        functools.partial(_combine_kernel, alpha=alpha),
        grid=(n // tm,),
        in_specs=[smem(), smem(), pl.BlockSpec((tm, D_MODEL), lambda i: (i, 0)),
                  pl.BlockSpec((tm, LANES), lambda i: (i, 0)),
                  pl.BlockSpec((1, D_MODEL), lambda i: (0, 0)), pl.BlockSpec((1, D_MODEL), lambda i: (0, 0)),
                  pl.BlockSpec(memory_space=pl.ANY)],
        out_specs=pl.BlockSpec((tm, D_MODEL), lambda i: (i, 0)),
        out_shape=jax.ShapeDtypeStruct((n, D_MODEL), F32),
        scratch_shapes=[pltpu.VMEM((tm, D_MODEL), F32), pltpu.VMEM((tm, D_MODEL), F32),
                        pltpu.SemaphoreType.DMA(())],
        compiler_params=_cparams(("arbitrary",)),
        name="moe_combine",
    )(pos0, pos1, x2d, gates, row(ln_g), row(ln_b), ys)


def _ffn_moe(x2d, router_w, router_b, w1, w3, w2, ln_g, ln_b, alpha, tm=1024):
    n = x2d.shape[0]
    tm = min(tm, n)
    meta, gates, totals = _router(x2d, router_w, router_b)
    counts = totals[0, :N_EXPERTS].astype(jnp.int32)
    padded = ((counts + tm - 1) // tm) * tm
    ends = jnp.cumsum(padded)
    starts = ends - padded
    pos0 = starts[meta[:, 0]] + meta[:, 2]
    pos1 = starts[meta[:, 1]] + meta[:, 3]
    n_sorted = 2 * n + N_EXPERTS * tm
    tile_start = jnp.arange(n_sorted // tm, dtype=jnp.int32) * tm
    tile_expert = jnp.minimum(jnp.sum((ends[None, :] <= tile_start[:, None]).astype(jnp.int32), axis=1),
                              N_EXPERTS - 1)
    n_used = (ends[-1:] // tm).astype(jnp.int32)
    xs = _dispatch(x2d, pos0, pos1, n_sorted)
    ys = _experts(xs, tile_expert, n_used, w1, w3, w2, tm)
    return _combine(x2d, ys, pos0, pos1, gates, ln_g, ln_b, alpha)


def _arrange_w_in(w_in_l):
    rwkv_cols = 3 * BRANCH + LORA_W
    conv0 = rwkv_cols
    fox0 = conv0 + 3 * BRANCH
    f0 = fox0 + 3 * BRANCH
    gate0 = f0 + N_HEADS
    zeros = jnp.zeros((D_MODEL, COL_CB - COL_LORA - LORA_W), w_in_l.dtype)
    w_all = jnp.concatenate([w_in_l[:, :rwkv_cols], zeros, w_in_l[:, conv0:f0], w_in_l[:, gate0:]], axis=1)
    return w_all.astype(BF16), w_in_l[:, f0:gate0]


def kernel(x, ln0_g, ln0_b, w_in, mu_shift, w0_decay, w2_decay, a0, w2_iclr, w2_gate, k_k, k_a, r_k, lnx_g, lnx_b, conv_w, b_forget, b_gate, w_up_rwkv, w_up_conv, w_up_attn, w_out, ln1_g, ln1_b, ln2_g, ln2_b, ffn_w1, ffn_w3, ffn_w2, router_w, router_b, moe_w1, moe_w3, moe_w2):
    bsz, t_len, _ = x.shape
    n = bsz * t_len
    depth = w_in.shape[0]
    alpha = (2 * depth) ** 0.25
    xs = _ln0(x.reshape(n, D_MODEL), ln0_g, ln0_b)
    for l in range(depth):
        w_all, w_f = _arrange_w_in(w_in[l])
        proj = _proj(xs, w_all)
        proj3 = proj.reshape(bsz, t_len, PROJ_COLS)
        o_a = _rwkv(proj3, mu_shift[l], w0_decay[l], w2_decay[l], a0[l], w2_iclr[l], w2_gate[l],
                    k_k[l], k_a[l], r_k[l].reshape(-1), lnx_g[l], lnx_b[l])
        cum_nat = _forget_cumsum(xs.reshape(bsz, t_len, D_MODEL), w_f, b_forget[l])
        o_c = _fox_attention(proj3, cum_nat)
        xs = _merge(o_a.reshape(n, BRANCH), proj, o_c.reshape(n, BRANCH), xs, conv_w[l], b_gate[l],
                    w_up_rwkv[l], w_up_conv[l], w_up_attn[l], w_out[l], ln1_g[l], ln1_b[l], alpha, t_len)
        i = l // 2
        if l % 2 == 0:
            xs = _ffn_dense(xs, ffn_w1[i], ffn_w3[i], ffn_w2[i], ln2_g[l], ln2_b[l], alpha)
        else:
            xs = _ffn_moe(xs, router_w[i], router_b[i], moe_w1[i], moe_w3[i], moe_w2[i],
                          ln2_g[l], ln2_b[l], alpha)
    return xs.reshape(bsz, t_len, D_MODEL)
```

```python
import functools
import math

import jax
import jax.numpy as jnp
from jax import lax
from jax.experimental import pallas as pl
from jax.experimental.pallas import tpu as pltpu

F32 = jnp.float32
BF16 = jnp.bfloat16

D_MODEL = 1024
BRANCH = 512
HEAD_DIM = 64
N_HEADS = BRANCH // HEAD_DIM
LORA_W = 256
N_EXPERTS = 8
LANES = 128
RWKV_LN_EPS = 64e-5
LN_EPS = 1e-5
DECAY_SCALE = math.exp(-0.5)

COL_R, COL_K, COL_V, COL_LORA = 0, 512, 1024, 1536
COL_CB, COL_CC, COL_CH = 2048, 2560, 3072
COL_FQ, COL_FK, COL_FV = 3584, 4096, 4608
COL_GATE = 5120
PROJ_COLS = 8192

RWKV_CHUNK = 64
GROUP_HEADS = 4
N_GROUPS = N_HEADS // GROUP_HEADS
CHUNKS_PER_ITER = 8
VMEM_LIMIT = 56 * 1024 * 1024


def _cparams(sem):
    return pltpu.CompilerParams(dimension_semantics=sem, vmem_limit_bytes=VMEM_LIMIT)


def _dot(a, b):
    return jnp.dot(a.astype(BF16), b.astype(BF16), preferred_element_type=F32)


def _dot_nt(a, b):
    return lax.dot_general(a.astype(BF16), b.astype(BF16), (((1,), (1,)), ((), ())),
                           preferred_element_type=F32)


def _split2(x):
    hi = x.astype(BF16)
    lo = (x - hi.astype(F32)).astype(BF16)
    return hi, lo


def _split3(x):
    x1 = x.astype(BF16)
    r1 = x - x1.astype(F32)
    x2 = r1.astype(BF16)
    x3 = (r1 - x2.astype(F32)).astype(BF16)
    return x1, x2, x3


def _layer_norm(x, g, b):
    mu = jnp.mean(x, axis=-1, keepdims=True)
    xc = x - mu
    var = jnp.mean(xc * xc, axis=-1, keepdims=True)
    return xc * lax.rsqrt(var + LN_EPS) * g + b


def _sigmoid(x):
    return 1.0 / (1.0 + jnp.exp(-x))


def _ln0_kernel(x_ref, g_ref, b_ref, o_ref):
    o_ref[...] = _layer_norm(x_ref[...], g_ref[...], b_ref[...])


def _ln0(x2d, g, b, tm=1024):
    n = x2d.shape[0]
    tm = min(tm, n)
    return pl.pallas_call(
        _ln0_kernel,
        grid=(n // tm,),
        in_specs=[pl.BlockSpec((tm, D_MODEL), lambda i: (i, 0)),
                  pl.BlockSpec((1, D_MODEL), lambda i: (0, 0)),
                  pl.BlockSpec((1, D_MODEL), lambda i: (0, 0))],
        out_specs=pl.BlockSpec((tm, D_MODEL), lambda i: (i, 0)),
        out_shape=jax.ShapeDtypeStruct((n, D_MODEL), F32),
        compiler_params=_cparams(("parallel",)),
        name="ln0",
    )(x2d, g.reshape(1, -1), b.reshape(1, -1))


def _proj_kernel(x_ref, w_ref, o_ref, xb_ref):
    @pl.when(pl.program_id(1) == 0)
    def _():
        xb_ref[...] = x_ref[...].astype(BF16)

    o_ref[...] = jnp.dot(xb_ref[...], w_ref[...], preferred_element_type=F32).astype(o_ref.dtype)


def _proj(x2d, w, tm=1024, tn=1024):
    n = x2d.shape[0]
    cols = w.shape[1]
    tm = min(tm, n)
    return pl.pallas_call(
        _proj_kernel,
        grid=(n // tm, cols // tn),
        in_specs=[pl.BlockSpec((tm, D_MODEL), lambda i, j: (i, 0)),
                  pl.BlockSpec((D_MODEL, tn), lambda i, j: (0, j))],
        out_specs=pl.BlockSpec((tm, tn), lambda i, j: (i, j)),
        out_shape=jax.ShapeDtypeStruct((n, cols), BF16),
        scratch_shapes=[pltpu.VMEM((tm, D_MODEL), BF16)],
        compiler_params=_cparams(("parallel", "arbitrary")),
        name="in_proj",
    )(x2d, w)


def _shifted(x, prev_row):
    rolled = pltpu.roll(x, 1, 0)
    row = lax.broadcasted_iota(jnp.int32, x.shape, 0)
    return jnp.where(row == 0, prev_row, rolled)


def _rwkv_kernel(r_ref, k_ref, v_ref, lo_ref, mur_ref, muk_ref, muv_ref, mulo_ref,
                 w0_ref, w2d_ref, a0_ref, w2i_ref, w2g_ref, kk_ref, ka_ref, rk_ref,
                 lng_ref, lnb_ref, o_ref,
                 prev_ref, prevlo_ref, h_ref, r_s, lw_s, k_s, v_s, a_s, b_s, y_s):
    tt = r_ref.shape[1]
    c = RWKV_CHUNK

    @pl.when(pl.program_id(1) == 0)
    def _():
        prev_ref[...] = jnp.zeros_like(prev_ref)
        prevlo_ref[...] = jnp.zeros_like(prevlo_ref)
        h_ref[...] = jnp.zeros_like(h_ref)

    def lerp(ref, mu_ref, p_ref, idx, width):
        x = ref[0].astype(F32)
        prev_row = p_ref[0:1, idx * BRANCH: idx * BRANCH + width]
        out = x + mu_ref[...] * (_shifted(x, prev_row) - x)
        p_ref[0:1, idx * BRANCH: idx * BRANCH + width] = x[tt - 1:tt, :]
        return out

    r = lerp(r_ref, mur_ref, prev_ref, 0, BRANCH)
    k = lerp(k_ref, muk_ref, prev_ref, 1, BRANCH)
    v = lerp(v_ref, muv_ref, prev_ref, 2, BRANCH)
    lo = lerp(lo_ref, mulo_ref, prevlo_ref, 0, LORA_W)

    w_raw = w0_ref[...] + _dot(jnp.tanh(lo[:, 0:64]), w2d_ref[...])
    lw = -DECAY_SCALE * _sigmoid(w_raw)
    a = _sigmoid(a0_ref[...] + _dot(lo[:, 64:128], w2i_ref[...]))
    g = _dot(_sigmoid(lo[:, 128:256]), w2g_ref[...])

    rowh = lax.broadcasted_iota(jnp.int32, (BRANCH, BRANCH), 0) // HEAD_DIM
    colh = lax.broadcasted_iota(jnp.int32, (BRANCH, BRANCH), 1) // HEAD_DIM
    head_ones = jnp.where(rowh == colh, 1.0, 0.0).astype(BF16)

    def head_sum(x):
        hi, lo_ = _split2(x)
        return (jnp.dot(hi, head_ones, preferred_element_type=F32)
                + jnp.dot(lo_, head_ones, preferred_element_type=F32))

    kk = k * kk_ref[...]
    kk_norm = jnp.sqrt(head_sum(kk * kk))
    kkn = kk / jnp.maximum(kk_norm, 1e-12)
    k2 = k * (1.0 + (a - 1.0) * ka_ref[...])
    bonus = head_sum(r * k2 * rk_ref[...]) * v

    r_s[...] = r
    lw_s[...] = lw
    k_s[...] = k2
    v_s[...] = v
    a_s[...] = -kkn
    b_s[...] = kkn * a

    ltri = jnp.where(lax.broadcasted_iota(jnp.int32, (c, c), 0)
                     >= lax.broadcasted_iota(jnp.int32, (c, c), 1), 1.0, 0.0).astype(BF16)

    gw = GROUP_HEADS * HEAD_DIM
    t_w = lax.broadcasted_iota(jnp.int32, (c, gw), 0)
    s_w = lax.broadcasted_iota(jnp.int32, (c, gw), 1) % HEAD_DIM
    strict_w = t_w > s_w
    incl_w = t_w >= s_w
    eye_w = jnp.where(t_w == s_w, 1.0, 0.0)
    eye_wb = eye_w.astype(BF16)
    bd_mask = (lax.broadcasted_iota(jnp.int32, (gw, gw), 0) // HEAD_DIM
               == lax.broadcasted_iota(jnp.int32, (gw, gw), 1) // HEAD_DIM)

    def bd(x):
        xb = x.astype(BF16)
        return jnp.where(bd_mask, jnp.concatenate([xb] * GROUP_HEADS, axis=0), jnp.zeros((), BF16))

    def mm(a, b_bd):
        return jnp.dot(a.astype(BF16), b_bd, preferred_element_type=F32)

    def mm_nt(a, b_bd):
        return lax.dot_general(a.astype(BF16), b_bd, (((1,), (1,)), ((), ())), preferred_element_type=F32)

    def chunk_static(rows, g):
        sl = slice(g * gw, (g + 1) * gw)
        lw_c = lw_s[rows, sl]
        lw_hi, lw_lo = _split2(lw_c)
        cum = (jnp.dot(ltri, lw_hi, preferred_element_type=F32)
               + jnp.dot(ltri, lw_lo, preferred_element_type=F32))
        cum_last = cum[c - 1:c, :]
        p_inv = jnp.exp(-cum)
        p_tail = jnp.exp(cum_last - cum)
        b_c, k_c = b_s[rows, sl], k_s[rows, sl]
        ar = jnp.concatenate([a_s[rows, sl] * jnp.exp(cum - lw_c), r_s[rows, sl] * jnp.exp(cum)],
                             axis=0).astype(BF16)
        yield
        aab = mm_nt(ar, bd(b_c * p_inv))
        aak = mm_nt(ar, bd(k_c * p_inv))
        b_t = mm_nt(eye_wb, bd(b_c * p_tail)).astype(BF16)
        k_t = mm_nt(eye_wb, bd(k_c * p_tail)).astype(BF16)
        pl_hi, pl_lo = _split2(jnp.broadcast_to(jnp.exp(cum_last), (c, gw)))
        decay_w = mm_nt(eye_wb, bd(pl_hi)) + mm_nt(eye_wb, bd(pl_lo))
        yield
        n_w = jnp.where(strict_w, aab[0:c], 0.0)
        a_ark = jnp.concatenate([jnp.where(strict_w, aak[0:c], 0.0),
                                 jnp.where(incl_w, aak[c:2 * c], 0.0)], axis=0).astype(BF16)
        a_rb = jnp.where(incl_w, aab[c:2 * c], 0.0).astype(BF16)
        tinv = eye_w + n_w
        npow = n_w
        for _ in range(5):
            npow = mm(npow, bd(npow))
            yield
            tinv = tinv + mm(npow, bd(tinv))
            yield
        return ar, a_ark, a_rb, tinv.astype(BF16), b_t, k_t, decay_w, bd(v_s[rows, sl])

    def chunk_state(rows, g, st):
        ar, a_ark, a_rb, tinv, b_t, k_t, decay_w, v_bd = st
        h0 = h_ref[g]
        arh = mm(ar, bd(h0))
        akv = mm(a_ark, v_bd)
        yield
        u_bd = bd(mm(tinv, bd(arh[0:c] + akv[0:c])))
        yield
        y_s[rows, g * gw:(g + 1) * gw] = arh[c:2 * c] + akv[c:2 * c] + mm(a_rb, u_bd)
        h_ref[g] = decay_w * h0 + mm(b_t, u_bd) + mm(k_t, v_bd)

    def in_lockstep(gens):
        results = [None] * len(gens)
        live = list(range(len(gens)))
        while live:
            for idx in list(live):
                try:
                    next(gens[idx])
                except StopIteration as stop:
                    results[idx] = stop.value
                    live.remove(idx)
        return results

    def chunk_body(ic, carry):
        items = [(pl.ds(pl.multiple_of((ic * CHUNKS_PER_ITER + j) * c, c), c), g)
                 for j in range(CHUNKS_PER_ITER) for g in range(N_GROUPS)]
        statics = in_lockstep([chunk_static(r, g) for r, g in items])
        for j in range(CHUNKS_PER_ITER):
            in_lockstep([chunk_state(*items[j * N_GROUPS + g], statics[j * N_GROUPS + g])
                         for g in range(N_GROUPS)])
        return carry

    lax.fori_loop(0, tt // (c * CHUNKS_PER_ITER), chunk_body, 0)

    y = y_s[...]
    mean = head_sum(y) * (1.0 / HEAD_DIM)
    yc = y - mean
    var = head_sum(yc * yc) * (1.0 / HEAD_DIM)
    yn = yc * lax.rsqrt(var + RWKV_LN_EPS) * lng_ref[...] + lnb_ref[...]
    o_ref[0] = ((yn + bonus) * g).astype(o_ref.dtype)


def _rwkv(proj3, mu_shift, w0, w2d, a0, w2i, w2g, k_k, k_a, r_k, lnx_g, lnx_b, tt=512):
    bsz, t_len, _ = proj3.shape
    tt = min(tt, t_len)
    row = lambda a: a.reshape(1, -1).astype(F32)
    mu_r, mu_k, mu_v = (row(mu_shift[i * BRANCH:(i + 1) * BRANCH]) for i in range(3))
    mu_lo = row(mu_shift[3 * BRANCH:])
    vec = lambda width: pl.BlockSpec((1, width), lambda b, t: (0, 0))
    mat = lambda shape: pl.BlockSpec(shape, lambda b, t: (0, 0))
    act = lambda col, width: pl.BlockSpec((1, tt, width), lambda b, t: (b, t, col // width))
    return pl.pallas_call(
        _rwkv_kernel,
        grid=(bsz, t_len // tt),
        in_specs=[act(COL_R, BRANCH), act(COL_K, BRANCH), act(COL_V, BRANCH), act(COL_LORA, LORA_W),
                  vec(BRANCH), vec(BRANCH), vec(BRANCH), vec(LORA_W),
                  vec(BRANCH), mat((64, BRANCH)), vec(BRANCH), mat((64, BRANCH)), mat((128, BRANCH)),
                  vec(BRANCH), vec(BRANCH), vec(BRANCH), vec(BRANCH), vec(BRANCH)],
        out_specs=pl.BlockSpec((1, tt, BRANCH), lambda b, t: (b, t, 0)),
        out_shape=jax.ShapeDtypeStruct((bsz, t_len, BRANCH), BF16),
        scratch_shapes=[pltpu.VMEM((8, 3 * BRANCH), F32), pltpu.VMEM((8, LORA_W), F32),
                        pltpu.VMEM((N_GROUPS, HEAD_DIM, GROUP_HEADS * HEAD_DIM), F32)]
                       + [pltpu.VMEM((tt, BRANCH), F32) for _ in range(7)],
        compiler_params=_cparams(("parallel", "arbitrary")),
        name="rwkv7",
    )(proj3, proj3, proj3, proj3, mu_r, mu_k, mu_v, mu_lo,
      row(w0), w2d.astype(BF16), row(a0), w2i.astype(BF16), w2g.astype(BF16),
      row(k_k), row(k_a), row(r_k), row(lnx_g), row(lnx_b))


def _forget_kernel(x_ref, wf_ref, bf_ref, nat_ref, carry_ref):
    tt = x_ref.shape[1]

    @pl.when(pl.program_id(1) == 0)
    def _():
        carry_ref[...] = jnp.zeros_like(carry_ref)

    f = jnp.dot(x_ref[0].astype(BF16), wf_ref[...], preferred_element_type=F32) + bf_ref[...]
    log_f = jnp.minimum(f, 0.0) - jnp.log1p(jnp.exp(-jnp.abs(f)))
    ri = lax.broadcasted_iota(jnp.int32, (tt, tt), 0)
    ci = lax.broadcasted_iota(jnp.int32, (tt, tt), 1)
    ltri = jnp.where(ri >= ci, 1.0, 0.0).astype(BF16)
    cum = carry_ref[0:1, :]
    for part in _split3(log_f):
        cum = cum + jnp.dot(ltri, part, preferred_element_type=F32)
    carry_ref[0:1, :] = cum[tt - 1:tt, :]
    nat_ref[0] = cum


def _forget_cumsum(x3, w_f, b_forget, tt=512):
    bsz, t_len, _ = x3.shape
    tt = min(tt, t_len)
    wf = jnp.zeros((D_MODEL, LANES), BF16).at[:, :N_HEADS].set(w_f.astype(BF16))
    bf = jnp.zeros((1, LANES), F32).at[0, :N_HEADS].set(b_forget)
    return pl.pallas_call(
        _forget_kernel,
        grid=(bsz, t_len // tt),
        in_specs=[pl.BlockSpec((1, tt, D_MODEL), lambda b, t: (b, t, 0)),
                  pl.BlockSpec((D_MODEL, LANES), lambda b, t: (0, 0)),
                  pl.BlockSpec((1, LANES), lambda b, t: (0, 0))],
        out_specs=pl.BlockSpec((1, tt, LANES), lambda b, t: (b, t, 0)),
        out_shape=jax.ShapeDtypeStruct((bsz, t_len, LANES), F32),
        scratch_shapes=[pltpu.VMEM((8, LANES), F32)],
        compiler_params=_cparams(("parallel", "arbitrary")),
        name="forget_cumsum",
    )(x3, wf, bf)


AUG_LANE = HEAD_DIM
VT_ROWS = HEAD_DIM + 16
KV_UNROLL = 2
SCORE_LOOKAHEAD = 2
LOG2E = math.log2(math.e)
UNDERFLOW_LOG2 = 152.0
NORM_SLACK = 1.01


def _bf16_pieces(x):
    p1 = x.astype(BF16).astype(F32)
    p2 = (x - p1).astype(BF16).astype(F32)
    p3 = (x - p1 - p2).astype(BF16).astype(F32)
    return p1, p2, p3


def _fox_prep_kernel(q_ref, k_ref, v_ref, cnat_ref, qa_ref, ka_ref, vt_ref, qn_ref, kn_ref):
    tt = q_ref.shape[1]
    q = q_ref[0].astype(F32) * (HEAD_DIM ** -0.5 * LOG2E)
    k = k_ref[0]
    v = v_ref[0]
    cnat = cnat_ref[0] * LOG2E
    lane = lax.broadcasted_iota(jnp.int32, (tt, LANES), 1)
    eye = jnp.where(lax.broadcasted_iota(jnp.int32, (HEAD_DIM, HEAD_DIM), 0)
                    == lax.broadcasted_iota(jnp.int32, (HEAD_DIM, HEAD_DIM), 1), 1.0, 0.0).astype(BF16)
    pad = jnp.zeros((tt, LANES - HEAD_DIM), F32)
    ones_rows = jnp.where(lax.broadcasted_iota(jnp.int32, (VT_ROWS - HEAD_DIM, tt), 0) == 0, 1.0, 0.0)

    head_cols = jnp.where(lax.broadcasted_iota(jnp.int32, (BRANCH, LANES), 0) // HEAD_DIM
                          == lax.broadcasted_iota(jnp.int32, (BRANCH, LANES), 1), 1.0, 0.0).astype(BF16)

    def max_sq_norms(x):
        return jnp.max(_dot(x * x, head_cols), axis=0, keepdims=True)

    q_rounded = q.astype(BF16).astype(F32)
    qn_ref[0, 0] = jnp.broadcast_to(max_sq_norms(q_rounded), (N_HEADS, LANES))
    kn_ref[0, 0] = jnp.broadcast_to(max_sq_norms(k.astype(F32)), (N_HEADS, LANES))
    for h in range(N_HEADS):
        sl = slice(h * HEAD_DIM, (h + 1) * HEAD_DIM)
        c1, c2, c3 = _bf16_pieces(cnat[:, h:h + 1])
        ones_q = (lane >= AUG_LANE + 3) & (lane < AUG_LANE + 6)
        aug_q = jnp.where(lane == AUG_LANE, c1, jnp.where(lane == AUG_LANE + 1, c2,
                          jnp.where(lane == AUG_LANE + 2, c3, jnp.where(ones_q, 1.0, 0.0))))
        q_h = q[:, sl].astype(BF16).astype(F32)
        k_h = k[:, sl].astype(F32)
        qa_ref[0, h] = jnp.where(lane < HEAD_DIM, jnp.concatenate([q_h, pad], axis=1), aug_q).astype(BF16)
        ones_k = (lane >= AUG_LANE) & (lane < AUG_LANE + 3)
        aug_k = jnp.where(lane == AUG_LANE + 3, -c1, jnp.where(lane == AUG_LANE + 4, -c2,
                          jnp.where(lane == AUG_LANE + 5, -c3, jnp.where(ones_k, 1.0, 0.0))))
        ka_ref[0, h] = jnp.where(lane < HEAD_DIM, jnp.concatenate([k_h, pad], axis=1), aug_k).astype(BF16)
        vt_ref[0, h, 0] = jnp.concatenate([_dot_nt(eye, v[:, sl]), ones_rows], axis=0).astype(BF16)


def _fox_prep(proj3, cum_nat, tt):
    bsz, t_len, _ = proj3.shape
    n_t = t_len // tt
    act = lambda col: pl.BlockSpec((1, tt, BRANCH), lambda b, t: (b, t, col // BRANCH))
    norm_spec = pl.BlockSpec((1, 1, N_HEADS, LANES), lambda b, t: (b, t, 0, 0))
    return pl.pallas_call(
        _fox_prep_kernel,
        grid=(bsz, n_t),
        in_specs=[act(COL_FQ), act(COL_FK), act(COL_FV),
                  pl.BlockSpec((1, tt, LANES), lambda b, t: (b, t, 0))],
        out_specs=[pl.BlockSpec((1, N_HEADS, tt, LANES), lambda b, t: (b, 0, t, 0)),
                   pl.BlockSpec((1, N_HEADS, tt, LANES), lambda b, t: (b, 0, t, 0)),
                   pl.BlockSpec((1, N_HEADS, 1, VT_ROWS, tt), lambda b, t: (b, 0, t, 0, 0)),
                   norm_spec, norm_spec],
        out_shape=[jax.ShapeDtypeStruct((bsz, N_HEADS, t_len, LANES), BF16),
                   jax.ShapeDtypeStruct((bsz, N_HEADS, t_len, LANES), BF16),
                   jax.ShapeDtypeStruct((bsz, N_HEADS, n_t, VT_ROWS, tt), BF16),
                   jax.ShapeDtypeStruct((bsz, n_t, N_HEADS, LANES), F32),
                   jax.ShapeDtypeStruct((bsz, n_t, N_HEADS, LANES), F32)],
        compiler_params=_cparams(("parallel", "parallel")),
        name="fox_prep",
    )(proj3, proj3, proj3, cum_nat)


def _fox_kernel(first_ref, qa_ref, ka_ref, vt_ref, o_ref, m_ref, acc_ref, *, tq, tk):
    b, hp, i = pl.program_id(0), pl.program_id(1), pl.program_id(2)
    n_chunks = tq // tk
    diag_mask = (lax.broadcasted_iota(jnp.int32, (tk, tk), 0)
                 <= lax.broadcasted_iota(jnp.int32, (tk, tk), 1))

    m_ref[...] = jnp.full(m_ref.shape, -1e30, F32)
    acc_ref[...] = jnp.zeros(acc_ref.shape, F32)

    def scores(j, hh, c):
        rows = pl.ds(pl.multiple_of(j * tk, tk), tk)
        return lax.dot_general(ka_ref[0, hh, rows, :], qa_ref[0, hh, c * tk:(c + 1) * tk, :],
                               (((1,), (1,)), ((), ())), preferred_element_type=F32)

    def update(j, hh, c, s, masked):
        if masked:
            s = jnp.where(diag_mask, s, -1e30)
        m = m_ref[hh, c, 0:1, :]
        m_new = jnp.maximum(m, jnp.max(s, axis=0, keepdims=True))
        alpha = jnp.exp2(m - m_new)
        p = jnp.exp2((s - m_new).astype(BF16))
        m_ref[hh, c, 0:1, :] = m_new
        acc_ref[hh, c] = alpha * acc_ref[hh, c] + jnp.dot(vt_ref[0, hh, j], p, preferred_element_type=F32)

    def step(j, hh, d):
        units = list(range(0 if d is None else d, n_chunks))
        pending = [scores(j, hh, c) for c in units[:SCORE_LOOKAHEAD]]
        for k, c in enumerate(units):
            if k + SCORE_LOOKAHEAD < len(units):
                pending.append(scores(j, hh, units[k + SCORE_LOOKAHEAD]))
            update(j, hh, c, pending.pop(0), c == d)

    n_full = i * n_chunks
    for hh in range(2):
        first = first_ref[(b * N_HEADS + hp * 2 + hh) * pl.num_programs(2) + i]

        def body(j2, carry, hh=hh):
            for u in range(KV_UNROLL):
                step(j2 * KV_UNROLL + u, hh, None)
            return carry

        lax.fori_loop(first // KV_UNROLL, n_full // KV_UNROLL, body, 0)
        for d in range(n_chunks):
            step(n_full + d, hh, d)
    out_t = jnp.concatenate(
        [jnp.concatenate([acc_ref[hh, c, 0:HEAD_DIM, :] / acc_ref[hh, c, HEAD_DIM:HEAD_DIM + 1, :]
                          for c in range(n_chunks)], axis=1) for hh in range(2)], axis=0)
    o_ref[0] = jnp.transpose(out_t).astype(o_ref.dtype)


def _first_key_blocks(cum_nat, q_norm2, k_norm2, tq, tk):
    q_max = jnp.max(q_norm2[:, :, 0, :N_HEADS], axis=1)
    k_max = jnp.max(k_norm2[:, :, 0, :N_HEADS], axis=1)
    bound = jnp.sqrt(q_max * k_max) * NORM_SLACK
    cum2 = cum_nat[:, :, :N_HEADS] * LOG2E
    cq_start = cum2[:, ::tq, :]
    ck_end = cum2[:, tk - 1::tk, :]
    bias_ub = cq_start[:, :, None, :] - ck_end[:, None, :, :]
    n_q, n_k = cq_start.shape[1], ck_end.shape[1]
    below_diag = jnp.arange(n_k)[None, :] < (jnp.arange(n_q) * (tq // tk))[:, None]
    dead = (bias_ub < -(2.0 * bound[:, None, None, :] + UNDERFLOW_LOG2)) & below_diag[None, :, :, None]
    first = jnp.sum(dead.astype(jnp.int32), axis=2)
    return jnp.transpose(first, (0, 2, 1)).reshape(-1)


def _fox_attention(proj3, cum_nat, tq=1024, tk=512):
    bsz, t_len, _ = proj3.shape
    tq = min(tq, t_len)
    tk = min(tk, tq)
    q_aug, k_aug, v_t, q_norm2, k_norm2 = _fox_prep(proj3, cum_nat, tk)
    first = _first_key_blocks(cum_nat, q_norm2, k_norm2, tq, tk)
    grid_spec = pltpu.PrefetchScalarGridSpec(
        num_scalar_prefetch=1,
        grid=(bsz, N_HEADS // 2, t_len // tq),
        in_specs=[pl.BlockSpec((1, 2, tq, LANES), lambda b, hp, i, first: (b, hp, i, 0)),
                  pl.BlockSpec((1, 2, t_len, LANES), lambda b, hp, i, first: (b, hp, 0, 0)),
                  pl.BlockSpec((1, 2, t_len // tk, VT_ROWS, tk), lambda b, hp, i, first: (b, hp, 0, 0, 0))],
        out_specs=pl.BlockSpec((1, tq, LANES), lambda b, hp, i, first: (b, i, hp)),
        scratch_shapes=[pltpu.VMEM((2, tq // tk, 8, tk), F32), pltpu.VMEM((2, tq // tk, VT_ROWS, tk), F32)])
    return pl.pallas_call(
        functools.partial(_fox_kernel, tq=tq, tk=tk),
        grid_spec=grid_spec,
        out_shape=jax.ShapeDtypeStruct((bsz, t_len, BRANCH), BF16),
        compiler_params=_cparams(("parallel", "parallel", "arbitrary")),
        name="fox_attention",
    )(first, q_aug, k_aug, v_t)


def _merge_kernel(oa_ref, cb_ref, cc_ref, ch_ref, ccp_ref, chp_ref, oc_ref,
                  g0_ref, g1_ref, g2_ref, x_ref, convw_ref, bg_ref,
                  wua_ref, wub_ref, wuc_ref, wo_ref, lg_ref, lb_ref, o_ref, *, alpha, tiles_per_seq):
    tm = x_ref.shape[0]
    first = (pl.program_id(0) % tiles_per_seq) == 0
    u = cc_ref[...].astype(F32) * ch_ref[...].astype(F32)
    u_prev = ccp_ref[...].astype(F32) * chp_ref[...].astype(F32)
    u_prev = jnp.where(first, 0.0, u_prev)
    row = lax.broadcasted_iota(jnp.int32, u.shape, 0)
    u1 = jnp.where(row == 0, u_prev[7:8, :], pltpu.roll(u, 1, 0))
    u2 = pltpu.roll(u, 2, 0)
    u2 = jnp.where(row == 0, u_prev[6:7, :], jnp.where(row == 1, u_prev[7:8, :], u2))
    cw = convw_ref[...]
    conv = cw[0:1, :] * u2 + cw[1:2, :] * u1 + cw[2:3, :] * u
    o_b = cb_ref[...].astype(F32) * conv

    bg = bg_ref[...]
    merged = _sigmoid(g0_ref[...].astype(F32) + bg[0:1, :]) * jnp.dot(
        oa_ref[...], wua_ref[...], preferred_element_type=F32)
    merged = merged + _sigmoid(g1_ref[...].astype(F32) + bg[1:2, :]) * _dot(o_b, wub_ref[...])
    merged = merged + _sigmoid(g2_ref[...].astype(F32) + bg[2:3, :]) * jnp.dot(
        oc_ref[...], wuc_ref[...], preferred_element_type=F32)
    mix = _dot(merged, wo_ref[...])
    o_ref[...] = _layer_norm(alpha * x_ref[...] + mix, lg_ref[...], lb_ref[...])


def _merge(o_a, proj, o_c, x2d, conv_w, b_gate, wua, wub, wuc, wo, ln_g, ln_b, alpha, t_len, tm=512):
    n = x2d.shape[0]
    tm = min(tm, t_len)
    tiles_per_seq = t_len // tm
    act = lambda col, width: pl.BlockSpec((tm, width), lambda i: (i, col // width))
    halo = lambda col: pl.BlockSpec(
        (8, BRANCH), lambda i: (jnp.maximum(i * (tm // 8) - 1, 0), col // BRANCH))
    full = lambda shape: pl.BlockSpec(shape, lambda i: (0, 0))
    row = lambda a: a.reshape(1, -1).astype(F32)
    return pl.pallas_call(
        functools.partial(_merge_kernel, alpha=alpha, tiles_per_seq=tiles_per_seq),
        grid=(n // tm,),
        in_specs=[act(0, BRANCH), act(COL_CB, BRANCH), act(COL_CC, BRANCH), act(COL_CH, BRANCH),
                  halo(COL_CC), halo(COL_CH), act(0, BRANCH),
                  act(COL_GATE, D_MODEL), act(COL_GATE + D_MODEL, D_MODEL),
                  act(COL_GATE + 2 * D_MODEL, D_MODEL),
                  act(0, D_MODEL), full((3, BRANCH)), full((3, D_MODEL)),
                  full((BRANCH, D_MODEL)), full((BRANCH, D_MODEL)), full((BRANCH, D_MODEL)),
                  full((D_MODEL, D_MODEL)), full((1, D_MODEL)), full((1, D_MODEL))],
        out_specs=pl.BlockSpec((tm, D_MODEL), lambda i: (i, 0)),
        out_shape=jax.ShapeDtypeStruct((n, D_MODEL), F32),
        compiler_params=_cparams(("parallel",)),
        name="merge",
    )(o_a, proj, proj, proj, proj, proj, o_c, proj, proj, proj, x2d,
      conv_w.astype(F32), b_gate.astype(F32),
      wua.astype(BF16), wub.astype(BF16), wuc.astype(BF16), wo.astype(BF16), row(ln_g), row(ln_b))


def _ffn_kernel(x_ref, w1_ref, w3_ref, w2_ref, lg_ref, lb_ref, o_ref, xb_ref, acc_ref, *, alpha):
    j = pl.program_id(1)

    @pl.when(j == 0)
    def _():
        xb_ref[...] = x_ref[...].astype(BF16)
        acc_ref[...] = jnp.zeros_like(acc_ref)

    xb = xb_ref[...]
    h1 = jnp.dot(xb, w1_ref[...], preferred_element_type=F32)
    h3 = jnp.dot(xb, w3_ref[...], preferred_element_type=F32)
    hidden = (h1 * _sigmoid(h1) * h3).astype(BF16)
    acc_ref[...] += jnp.dot(hidden, w2_ref[...], preferred_element_type=F32)

    @pl.when(j == pl.num_programs(1) - 1)
    def _():
        o_ref[...] = _layer_norm(alpha * x_ref[...] + acc_ref[...], lg_ref[...], lb_ref[...])


def _ffn_dense(x2d, w1, w3, w2, ln_g, ln_b, alpha, tm=1024, tf=256):
    n = x2d.shape[0]
    tm = min(tm, n)
    d_ff = w1.shape[1]
    row = lambda a: a.reshape(1, -1).astype(F32)
    return pl.pallas_call(
        functools.partial(_ffn_kernel, alpha=alpha),
        grid=(n // tm, d_ff // tf),
        in_specs=[pl.BlockSpec((tm, D_MODEL), lambda i, j: (i, 0)),
                  pl.BlockSpec((D_MODEL, tf), lambda i, j: (0, j)),
                  pl.BlockSpec((D_MODEL, tf), lambda i, j: (0, j)),
                  pl.BlockSpec((tf, D_MODEL), lambda i, j: (j, 0)),
                  pl.BlockSpec((1, D_MODEL), lambda i, j: (0, 0)),
                  pl.BlockSpec((1, D_MODEL), lambda i, j: (0, 0))],
        out_specs=pl.BlockSpec((tm, D_MODEL), lambda i, j: (i, 0)),
        out_shape=jax.ShapeDtypeStruct((n, D_MODEL), F32),
        scratch_shapes=[pltpu.VMEM((tm, D_MODEL), BF16), pltpu.VMEM((tm, D_MODEL), F32)],
        compiler_params=_cparams(("parallel", "arbitrary")),
        name="ffn_dense",
    )(x2d, w1.astype(BF16), w3.astype(BF16), w2.astype(BF16), row(ln_g), row(ln_b))


def _router_kernel(x_ref, w_ref, b_ref, meta_ref, gate_ref, total_ref, count_ref):
    @pl.when(pl.program_id(0) == 0)
    def _():
        count_ref[...] = jnp.zeros_like(count_ref)

    x = x_ref[...]
    tm = x.shape[0]
    logits = b_ref[...] + jnp.zeros((tm, LANES), F32)
    w_parts = (w_ref[0], w_ref[1], w_ref[2])
    x_parts = _split3(x)
    for xi in range(2):
        for wi in range(2 - xi):
            logits = logits + jnp.dot(x_parts[xi], w_parts[wi], preferred_element_type=F32)
    lane = lax.broadcasted_iota(jnp.int32, (tm, LANES), 1)
    neg = -1e30
    logits = jnp.where(lane < N_EXPERTS, logits, neg)
    m1 = jnp.max(logits, axis=1, keepdims=True)
    i1 = jnp.min(jnp.where(logits == m1, lane, LANES), axis=1, keepdims=True)
    rest = jnp.where(lane == i1, neg, logits)
    m2 = jnp.max(rest, axis=1, keepdims=True)
    i2 = jnp.min(jnp.where(rest == m2, lane, LANES), axis=1, keepdims=True)
    e2 = jnp.exp(m2 - m1)
    g1 = 1.0 / (1.0 + e2)
    g2 = e2 / (1.0 + e2)
    chosen = jnp.where((lane == i1) | (lane == i2), 1.0, 0.0)
    lstrict = jnp.where(lax.broadcasted_iota(jnp.int32, (tm, tm), 0)
                        > lax.broadcasted_iota(jnp.int32, (tm, tm), 1), 1.0, 0.0).astype(BF16)
    before = count_ref[0:1, :] + jnp.dot(lstrict, chosen.astype(BF16), preferred_element_type=F32)
    count_ref[0:1, :] = before[tm - 1:tm, :] + chosen[tm - 1:tm, :]
    total_ref[...] = jnp.broadcast_to(count_ref[0:1, :], total_ref.shape)
    rank1 = jnp.sum(jnp.where(lane == i1, before, 0.0), axis=1, keepdims=True)
    rank2 = jnp.sum(jnp.where(lane == i2, before, 0.0), axis=1, keepdims=True)
    meta = jnp.where(lane == 0, i1.astype(F32), jnp.where(lane == 1, i2.astype(F32),
                     jnp.where(lane == 2, rank1, jnp.where(lane == 3, rank2, 0.0))))
    meta_ref[...] = meta.astype(jnp.int32)
    gate_ref[...] = jnp.where(lane == 0, g1, jnp.where(lane == 1, g2, 0.0))


def _router(x2d, router_w, router_b, tm=512):
    n = x2d.shape[0]
    tm = min(tm, n)
    w_pad = jnp.zeros((D_MODEL, LANES), F32).at[:, :N_EXPERTS].set(router_w)
    w_parts = jnp.stack(_split3(w_pad))
    b_pad = jnp.zeros((1, LANES), F32).at[0, :N_EXPERTS].set(router_b)
    return pl.pallas_call(
        _router_kernel,
        grid=(n // tm,),
        in_specs=[pl.BlockSpec((tm, D_MODEL), lambda i: (i, 0)),
                  pl.BlockSpec((3, D_MODEL, LANES), lambda i: (0, 0, 0)),
                  pl.BlockSpec((1, LANES), lambda i: (0, 0))],
        out_specs=[pl.BlockSpec((tm, LANES), lambda i: (i, 0)),
                   pl.BlockSpec((tm, LANES), lambda i: (i, 0)),
                   pl.BlockSpec((8, LANES), lambda i: (0, 0))],
        out_shape=[jax.ShapeDtypeStruct((n, LANES), jnp.int32),
                   jax.ShapeDtypeStruct((n, LANES), F32),
                   jax.ShapeDtypeStruct((8, LANES), F32)],
        scratch_shapes=[pltpu.VMEM((8, LANES), F32)],
        compiler_params=_cparams(("arbitrary",)),
        name="router",
    )(x2d, w_parts, b_pad)


DMA_UNROLL = 8


def _row_copies(src_ref, dst_ref, sem, src_row, dst_row):
    return pltpu.make_async_copy(src_ref.at[pl.ds(src_row, 1)], dst_ref.at[pl.ds(dst_row, 1)], sem)


def _dispatch_kernel(pos0_ref, pos1_ref, x_ref, zero_ref, xs_ref, sem):
    del zero_ref
    tm = x_ref.shape[0]

    def copies(r):
        return (_row_copies(x_ref, xs_ref, sem, r, pos0_ref[r]),
                _row_copies(x_ref, xs_ref, sem, r, pos1_ref[r]))

    def start(r, carry):
        for cp in copies(r):
            cp.start()
        return carry

    def wait(r, carry):
        for cp in copies(r):
            cp.wait()
        return carry

    lax.fori_loop(0, tm, start, 0, unroll=DMA_UNROLL)
    lax.fori_loop(0, tm, wait, 0, unroll=DMA_UNROLL)


def _dispatch(x2d, pos0, pos1, n_sorted, tm=512):
    n = x2d.shape[0]
    tm = min(tm, n)
    smem = lambda: pl.BlockSpec((tm,), lambda i: (i,), memory_space=pltpu.SMEM)
    return pl.pallas_call(
        _dispatch_kernel,
        grid=(n // tm,),
        in_specs=[smem(), smem(), pl.BlockSpec((tm, D_MODEL), lambda i: (i, 0)),
                  pl.BlockSpec(memory_space=pl.ANY)],
        out_specs=pl.BlockSpec(memory_space=pl.ANY),
        out_shape=jax.ShapeDtypeStruct((n_sorted, D_MODEL), F32),
        scratch_shapes=[pltpu.SemaphoreType.DMA(())],
        input_output_aliases={3: 0},
        compiler_params=_cparams(("arbitrary",)),
        name="moe_dispatch",
    )(pos0, pos1, x2d, jnp.zeros((n_sorted, D_MODEL), F32))


def _experts_kernel(te_ref, used_ref, x_ref, w1_ref, w3_ref, w2_ref, o_ref, xb_ref, acc_ref):
    i = pl.program_id(0)
    j = pl.program_id(1)
    last = pl.num_programs(1) - 1
    live = i < used_ref[0]

    @pl.when(live & (j == 0))
    def _():
        xb_ref[...] = x_ref[...].astype(BF16)
        acc_ref[...] = jnp.zeros_like(acc_ref)

    @pl.when(live)
    def _():
        xb = xb_ref[...]
        h1 = jnp.dot(xb, w1_ref[0].astype(BF16), preferred_element_type=F32)
        h3 = jnp.dot(xb, w3_ref[0].astype(BF16), preferred_element_type=F32)
        hidden = (h1 * _sigmoid(h1) * h3).astype(BF16)
        acc_ref[...] += jnp.dot(hidden, w2_ref[0].astype(BF16), preferred_element_type=F32)

    @pl.when(live & (j == last))
    def _():
        o_ref[...] = acc_ref[...]

    @pl.when(jnp.logical_not(live) & (j == last))
    def _():
        o_ref[...] = jnp.zeros_like(o_ref)


def _experts(xs, tile_expert, n_used, w1, w3, w2, tm, tf=512):
    n_sorted = xs.shape[0]
    d_ff = w1.shape[2]
    n_f = d_ff // tf
    def jj(i, j, used):
        return jnp.where(i < used[0], j, n_f - 1)
    grid_spec = pltpu.PrefetchScalarGridSpec(
        num_scalar_prefetch=2,
        grid=(n_sorted // tm, n_f),
        in_specs=[pl.BlockSpec((tm, D_MODEL), lambda i, j, te, used: (jnp.minimum(i, used[0] - 1), 0)),
                  pl.BlockSpec((1, D_MODEL, tf), lambda i, j, te, used: (te[i], 0, jj(i, j, used))),
                  pl.BlockSpec((1, D_MODEL, tf), lambda i, j, te, used: (te[i], 0, jj(i, j, used))),
                  pl.BlockSpec((1, tf, D_MODEL), lambda i, j, te, used: (te[i], jj(i, j, used), 0))],
        out_specs=pl.BlockSpec((tm, D_MODEL), lambda i, j, te, used: (i, 0)),
        scratch_shapes=[pltpu.VMEM((tm, D_MODEL), BF16), pltpu.VMEM((tm, D_MODEL), F32)])
    return pl.pallas_call(
        _experts_kernel,
        grid_spec=grid_spec,
        out_shape=jax.ShapeDtypeStruct((n_sorted, D_MODEL), F32),
        compiler_params=_cparams(("arbitrary", "arbitrary")),
        name="moe_experts",
    )(tile_expert, n_used, xs, w1, w3, w2)


def _combine_kernel(pos0_ref, pos1_ref, x_ref, gate_ref, lg_ref, lb_ref, ys_ref, o_ref, y0_ref, y1_ref, sem,
                    *, alpha):
    tm = x_ref.shape[0]

    def copies(r):
        return (_row_copies(ys_ref, y0_ref, sem, pos0_ref[r], r),
                _row_copies(ys_ref, y1_ref, sem, pos1_ref[r], r))

    def start(r, carry):
        for cp in copies(r):
            cp.start()
        return carry

    def wait(r, carry):
        for cp in copies(r):
            cp.wait()
        return carry

    lax.fori_loop(0, tm, start, 0, unroll=DMA_UNROLL)
    lax.fori_loop(0, tm, wait, 0, unroll=DMA_UNROLL)
    gates = gate_ref[...]
    ffn = gates[:, 0:1] * y0_ref[...] + gates[:, 1:2] * y1_ref[...]
    o_ref[...] = _layer_norm(alpha * x_ref[...] + ffn, lg_ref[...], lb_ref[...])


def _combine(x2d, ys, pos0, pos1, gates, ln_g, ln_b, alpha, tm=256):
    n = x2d.shape[0]
    tm = min(tm, n)
    smem = lambda: pl.BlockSpec((tm,), lambda i: (i,), memory_space=pltpu.SMEM)
    row = lambda a: a.reshape(1, -1).astype(F32)
    return pl.pallas_call(
        functools.partial(_combine_kernel, alpha=alpha),
        grid=(n // tm,),
        in_specs=[smem(), smem(), pl.BlockSpec((tm, D_MODEL), lambda i: (i, 0)),
                  pl.BlockSpec((tm, LANES), lambda i: (i, 0)),
                  pl.BlockSpec((1, D_MODEL), lambda i: (0, 0)), pl.BlockSpec((1, D_MODEL), lambda i: (0, 0)),
                  pl.BlockSpec(memory_space=pl.ANY)],
        out_specs=pl.BlockSpec((tm, D_MODEL), lambda i: (i, 0)),
        out_shape=jax.ShapeDtypeStruct((n, D_MODEL), F32),
        scratch_shapes=[pltpu.VMEM((tm, D_MODEL), F32), pltpu.VMEM((tm, D_MODEL), F32),
                        pltpu.SemaphoreType.DMA(())],
        compiler_params=_cparams(("arbitrary",)),
        name="moe_combine",
    )(pos0, pos1, x2d, gates, row(ln_g), row(ln_b), ys)


def _ffn_moe(x2d, router_w, router_b, w1, w3, w2, ln_g, ln_b, alpha, tm=1024):
    n = x2d.shape[0]
    tm = min(tm, n)
    meta, gates, totals = _router(x2d, router_w, router_b)
    counts = totals[0, :N_EXPERTS].astype(jnp.int32)
    padded = ((counts + tm - 1) // tm) * tm
    ends = jnp.cumsum(padded)
    starts = ends - padded
    pos0 = starts[meta[:, 0]] + meta[:, 2]
    pos1 = starts[meta[:, 1]] + meta[:, 3]
    n_sorted = 2 * n + N_EXPERTS * tm
    tile_start = jnp.arange(n_sorted // tm, dtype=jnp.int32) * tm
    tile_expert = jnp.minimum(jnp.sum((ends[None, :] <= tile_start[:, None]).astype(jnp.int32), axis=1),
                              N_EXPERTS - 1)
    n_used = (ends[-1:] // tm).astype(jnp.int32)
    xs = _dispatch(x2d, pos0, pos1, n_sorted)
    ys = _experts(xs, tile_expert, n_used, w1, w3, w2, tm)
    return _combine(x2d, ys, pos0, pos1, gates, ln_g, ln_b, alpha)


def _arrange_w_in(w_in_l):
    rwkv_cols = 3 * BRANCH + LORA_W
    conv0 = rwkv_cols
    fox0 = conv0 + 3 * BRANCH
    f0 = fox0 + 3 * BRANCH
    gate0 = f0 + N_HEADS
    zeros = jnp.zeros((D_MODEL, COL_CB - COL_LORA - LORA_W), w_in_l.dtype)
    w_all = jnp.concatenate([w_in_l[:, :rwkv_cols], zeros, w_in_l[:, conv0:f0], w_in_l[:, gate0:]], axis=1)
    return w_all.astype(BF16), w_in_l[:, f0:gate0]


def kernel(x, ln0_g, ln0_b, w_in, mu_shift, w0_decay, w2_decay, a0, w2_iclr, w2_gate, k_k, k_a, r_k, lnx_g, lnx_b, conv_w, b_forget, b_gate, w_up_rwkv, w_up_conv, w_up_attn, w_out, ln1_g, ln1_b, ln2_g, ln2_b, ffn_w1, ffn_w3, ffn_w2, router_w, router_b, moe_w1, moe_w3, moe_w2):
    bsz, t_len, _ = x.shape
    n = bsz * t_len
    depth = w_in.shape[0]
    alpha = (2 * depth) ** 0.25
    xs = _ln0(x.reshape(n, D_MODEL), ln0_g, ln0_b)
    for l in range(depth):
        w_all, w_f = _arrange_w_in(w_in[l])
        proj = _proj(xs, w_all)
        proj3 = proj.reshape(bsz, t_len, PROJ_COLS)
        o_a = _rwkv(proj3, mu_shift[l], w0_decay[l], w2_decay[l], a0[l], w2_iclr[l], w2_gate[l],
                    k_k[l], k_a[l], r_k[l].reshape(-1), lnx_g[l], lnx_b[l])
        cum_nat = _forget_cumsum(xs.reshape(bsz, t_len, D_MODEL), w_f, b_forget[l])
        o_c = _fox_attention(proj3, cum_nat)
        xs = _merge(o_a.reshape(n, BRANCH), proj, o_c.reshape(n, BRANCH), xs, conv_w[l], b_gate[l],
                    w_up_rwkv[l], w_up_conv[l], w_up_attn[l], w_out[l], ln1_g[l], ln1_b[l], alpha, t_len)
        i = l // 2
        if l % 2 == 0:
            xs = _ffn_dense(xs, ffn_w1[i], ffn_w3[i], ffn_w2[i], ln2_g[l], ln2_b[l], alpha)
        else:
            xs = _ffn_moe(xs, router_w[i], router_b[i], moe_w1[i], moe_w3[i], moe_w2[i],
                          ln2_g[l], ln2_b[l], alpha)
    return xs.reshape(bsz, t_len, D_MODEL)
```

```python
import functools
import math

import jax
import jax.numpy as jnp
from jax import lax
from jax.experimental import pallas as pl
from jax.experimental.pallas import tpu as pltpu

F32 = jnp.float32
BF16 = jnp.bfloat16

D_MODEL = 1024
BRANCH = 512
HEAD_DIM = 64
N_HEADS = BRANCH // HEAD_DIM
LORA_W = 256
N_EXPERTS = 8
LANES = 128
RWKV_LN_EPS = 64e-5
LN_EPS = 1e-5
DECAY_SCALE = math.exp(-0.5)

COL_R, COL_K, COL_V, COL_LORA = 0, 512, 1024, 1536
COL_CB, COL_CC, COL_CH = 2048, 2560, 3072
COL_FQ, COL_FK, COL_FV = 3584, 4096, 4608
COL_GATE = 5120
PROJ_COLS = 8192

RWKV_CHUNK = 64
GROUP_HEADS = 4
N_GROUPS = N_HEADS // GROUP_HEADS
CHUNKS_PER_ITER = 8
VMEM_LIMIT = 56 * 1024 * 1024


def _cparams(sem):
    return pltpu.CompilerParams(dimension_semantics=sem, vmem_limit_bytes=VMEM_LIMIT)


def _dot(a, b):
    return jnp.dot(a.astype(BF16), b.astype(BF16), preferred_element_type=F32)


def _dot_nt(a, b):
    return lax.dot_general(a.astype(BF16), b.astype(BF16), (((1,), (1,)), ((), ())),
                           preferred_element_type=F32)


def _split2(x):
    hi = x.astype(BF16)
    lo = (x - hi.astype(F32)).astype(BF16)
    return hi, lo


def _split3(x):
    x1 = x.astype(BF16)
    r1 = x - x1.astype(F32)
    x2 = r1.astype(BF16)
    x3 = (r1 - x2.astype(F32)).astype(BF16)
    return x1, x2, x3


def _layer_norm(x, g, b):
    mu = jnp.mean(x, axis=-1, keepdims=True)
    xc = x - mu
    var = jnp.mean(xc * xc, axis=-1, keepdims=True)
    return xc * lax.rsqrt(var + LN_EPS) * g + b


def _sigmoid(x):
    return 1.0 / (1.0 + jnp.exp(-x))


def _ln0_kernel(x_ref, g_ref, b_ref, o_ref):
    o_ref[...] = _layer_norm(x_ref[...], g_ref[...], b_ref[...])


def _ln0(x2d, g, b, tm=1024):
    n = x2d.shape[0]
    tm = min(tm, n)
    return pl.pallas_call(
        _ln0_kernel,
        grid=(n // tm,),
        in_specs=[pl.BlockSpec((tm, D_MODEL), lambda i: (i, 0)),
                  pl.BlockSpec((1, D_MODEL), lambda i: (0, 0)),
                  pl.BlockSpec((1, D_MODEL), lambda i: (0, 0))],
        out_specs=pl.BlockSpec((tm, D_MODEL), lambda i: (i, 0)),
        out_shape=jax.ShapeDtypeStruct((n, D_MODEL), F32),
        compiler_params=_cparams(("parallel",)),
        name="ln0",
    )(x2d, g.reshape(1, -1), b.reshape(1, -1))


def _proj_kernel(x_ref, w_ref, o_ref, xb_ref):
    @pl.when(pl.program_id(1) == 0)
    def _():
        xb_ref[...] = x_ref[...].astype(BF16)

    o_ref[...] = jnp.dot(xb_ref[...], w_ref[...], preferred_element_type=F32).astype(o_ref.dtype)


def _proj(x2d, w, tm=1024, tn=1024):
    n = x2d.shape[0]
    cols = w.shape[1]
    tm = min(tm, n)
    return pl.pallas_call(
        _proj_kernel,
        grid=(n // tm, cols // tn),
        in_specs=[pl.BlockSpec((tm, D_MODEL), lambda i, j: (i, 0)),
                  pl.BlockSpec((D_MODEL, tn), lambda i, j: (0, j))],
        out_specs=pl.BlockSpec((tm, tn), lambda i, j: (i, j)),
        out_shape=jax.ShapeDtypeStruct((n, cols), BF16),
        scratch_shapes=[pltpu.VMEM((tm, D_MODEL), BF16)],
        compiler_params=_cparams(("parallel", "arbitrary")),
        name="in_proj",
    )(x2d, w)


def _shifted(x, prev_row):
    rolled = pltpu.roll(x, 1, 0)
    row = lax.broadcasted_iota(jnp.int32, x.shape, 0)
    return jnp.where(row == 0, prev_row, rolled)


def _rwkv_kernel(r_ref, k_ref, v_ref, lo_ref, mur_ref, muk_ref, muv_ref, mulo_ref,
                 w0_ref, w2d_ref, a0_ref, w2i_ref, w2g_ref, kk_ref, ka_ref, rk_ref,
                 lng_ref, lnb_ref, o_ref,
                 prev_ref, prevlo_ref, h_ref, r_s, lw_s, k_s, v_s, a_s, b_s, y_s):
    tt = r_ref.shape[1]
    c = RWKV_CHUNK

    @pl.when(pl.program_id(1) == 0)
    def _():
        prev_ref[...] = jnp.zeros_like(prev_ref)
        prevlo_ref[...] = jnp.zeros_like(prevlo_ref)
        h_ref[...] = jnp.zeros_like(h_ref)

    def lerp(ref, mu_ref, p_ref, idx, width):
        x = ref[0].astype(F32)
        prev_row = p_ref[0:1, idx * BRANCH: idx * BRANCH + width]
        out = x + mu_ref[...] * (_shifted(x, prev_row) - x)
        p_ref[0:1, idx * BRANCH: idx * BRANCH + width] = x[tt - 1:tt, :]
        return out

    r = lerp(r_ref, mur_ref, prev_ref, 0, BRANCH)
    k = lerp(k_ref, muk_ref, prev_ref, 1, BRANCH)
    v = lerp(v_ref, muv_ref, prev_ref, 2, BRANCH)
    lo = lerp(lo_ref, mulo_ref, prevlo_ref, 0, LORA_W)

    w_raw = w0_ref[...] + _dot(jnp.tanh(lo[:, 0:64]), w2d_ref[...])
    lw = -DECAY_SCALE * _sigmoid(w_raw)
    a = _sigmoid(a0_ref[...] + _dot(lo[:, 64:128], w2i_ref[...]))
    g = _dot(_sigmoid(lo[:, 128:256]), w2g_ref[...])

    rowh = lax.broadcasted_iota(jnp.int32, (BRANCH, BRANCH), 0) // HEAD_DIM
    colh = lax.broadcasted_iota(jnp.int32, (BRANCH, BRANCH), 1) // HEAD_DIM
    head_ones = jnp.where(rowh == colh, 1.0, 0.0).astype(BF16)

    def head_sum(x):
        hi, lo_ = _split2(x)
        return (jnp.dot(hi, head_ones, preferred_element_type=F32)
                + jnp.dot(lo_, head_ones, preferred_element_type=F32))

    kk = k * kk_ref[...]
    kk_norm = jnp.sqrt(head_sum(kk * kk))
    kkn = kk / jnp.maximum(kk_norm, 1e-12)
    k2 = k * (1.0 + (a - 1.0) * ka_ref[...])
    bonus = head_sum(r * k2 * rk_ref[...]) * v

    r_s[...] = r
    lw_s[...] = lw
    k_s[...] = k2
    v_s[...] = v
    a_s[...] = -kkn
    b_s[...] = kkn * a

    ltri = jnp.where(lax.broadcasted_iota(jnp.int32, (c, c), 0)
                     >= lax.broadcasted_iota(jnp.int32, (c, c), 1), 1.0, 0.0).astype(BF16)

    gw = GROUP_HEADS * HEAD_DIM
    t_w = lax.broadcasted_iota(jnp.int32, (c, gw), 0)
    s_w = lax.broadcasted_iota(jnp.int32, (c, gw), 1) % HEAD_DIM
    strict_w = t_w > s_w
    incl_w = t_w >= s_w
    eye_w = jnp.where(t_w == s_w, 1.0, 0.0)
    eye_wb = eye_w.astype(BF16)
    bd_mask = (lax.broadcasted_iota(jnp.int32, (gw, gw), 0) // HEAD_DIM
               == lax.broadcasted_iota(jnp.int32, (gw, gw), 1) // HEAD_DIM)

    def bd(x):
        xb = x.astype(BF16)
        return jnp.where(bd_mask, jnp.concatenate([xb] * GROUP_HEADS, axis=0), jnp.zeros((), BF16))

    def mm(a, b_bd):
        return jnp.dot(a.astype(BF16), b_bd, preferred_element_type=F32)

    def mm_nt(a, b_bd):
        return lax.dot_general(a.astype(BF16), b_bd, (((1,), (1,)), ((), ())), preferred_element_type=F32)

    def chunk_static(rows, g):
        sl = slice(g * gw, (g + 1) * gw)
        lw_c = lw_s[rows, sl]
        lw_hi, lw_lo = _split2(lw_c)
        cum = (jnp.dot(ltri, lw_hi, preferred_element_type=F32)
               + jnp.dot(ltri, lw_lo, preferred_element_type=F32))
        cum_last = cum[c - 1:c, :]
        p_inv = jnp.exp(-cum)
        p_tail = jnp.exp(cum_last - cum)
        b_c, k_c = b_s[rows, sl], k_s[rows, sl]
        ar = jnp.concatenate([a_s[rows, sl] * jnp.exp(cum - lw_c), r_s[rows, sl] * jnp.exp(cum)],
                             axis=0).astype(BF16)
        yield
        aab = mm_nt(ar, bd(b_c * p_inv))
        aak = mm_nt(ar, bd(k_c * p_inv))
        b_t = mm_nt(eye_wb, bd(b_c * p_tail)).astype(BF16)
        k_t = mm_nt(eye_wb, bd(k_c * p_tail)).astype(BF16)
        pl_hi, pl_lo = _split2(jnp.broadcast_to(jnp.exp(cum_last), (c, gw)))
        decay_w = mm_nt(eye_wb, bd(pl_hi)) + mm_nt(eye_wb, bd(pl_lo))
        yield
        n_w = jnp.where(strict_w, aab[0:c], 0.0)
        a_ark = jnp.concatenate([jnp.where(strict_w, aak[0:c], 0.0),
                                 jnp.where(incl_w, aak[c:2 * c], 0.0)], axis=0).astype(BF16)
        a_rb = jnp.where(incl_w, aab[c:2 * c], 0.0).astype(BF16)
        tinv = eye_w + n_w
        npow = n_w
        for _ in range(5):
            npow = mm(npow, bd(npow))
            yield
            tinv = tinv + mm(npow, bd(tinv))
            yield
        return ar, a_ark, a_rb, tinv.astype(BF16), b_t, k_t, decay_w, bd(v_s[rows, sl])

    def chunk_state(rows, g, st):
        ar, a_ark, a_rb, tinv, b_t, k_t, decay_w, v_bd = st
        h0 = h_ref[g]
        arh = mm(ar, bd(h0))
        akv = mm(a_ark, v_bd)
        yield
        u_bd = bd(mm(tinv, bd(arh[0:c] + akv[0:c])))
        yield
        y_s[rows, g * gw:(g + 1) * gw] = arh[c:2 * c] + akv[c:2 * c] + mm(a_rb, u_bd)
        h_ref[g] = decay_w * h0 + mm(b_t, u_bd) + mm(k_t, v_bd)

    def in_lockstep(gens):
        results = [None] * len(gens)
        live = list(range(len(gens)))
        while live:
            for idx in list(live):
                try:
                    next(gens[idx])
                except StopIteration as stop:
                    results[idx] = stop.value
                    live.remove(idx)
        return results

    def chunk_body(ic, carry):
        items = [(pl.ds(pl.multiple_of((ic * CHUNKS_PER_ITER + j) * c, c), c), g)
                 for j in range(CHUNKS_PER_ITER) for g in range(N_GROUPS)]
        statics = in_lockstep([chunk_static(r, g) for r, g in items])
        for j in range(CHUNKS_PER_ITER):
            in_lockstep([chunk_state(*items[j * N_GROUPS + g], statics[j * N_GROUPS + g])
                         for g in range(N_GROUPS)])
        return carry

    lax.fori_loop(0, tt // (c * CHUNKS_PER_ITER), chunk_body, 0)

    y = y_s[...]
    mean = head_sum(y) * (1.0 / HEAD_DIM)
    yc = y - mean
    var = head_sum(yc * yc) * (1.0 / HEAD_DIM)
    yn = yc * lax.rsqrt(var + RWKV_LN_EPS) * lng_ref[...] + lnb_ref[...]
    o_ref[0] = ((yn + bonus) * g).astype(o_ref.dtype)


def _rwkv(proj3, mu_shift, w0, w2d, a0, w2i, w2g, k_k, k_a, r_k, lnx_g, lnx_b, tt=512):
    bsz, t_len, _ = proj3.shape
    tt = min(tt, t_len)
    row = lambda a: a.reshape(1, -1).astype(F32)
    mu_r, mu_k, mu_v = (row(mu_shift[i * BRANCH:(i + 1) * BRANCH]) for i in range(3))
    mu_lo = row(mu_shift[3 * BRANCH:])
    vec = lambda width: pl.BlockSpec((1, width), lambda b, t: (0, 0))
    mat = lambda shape: pl.BlockSpec(shape, lambda b, t: (0, 0))
    act = lambda col, width: pl.BlockSpec((1, tt, width), lambda b, t: (b, t, col // width))
    return pl.pallas_call(
        _rwkv_kernel,
        grid=(bsz, t_len // tt),
        in_specs=[act(COL_R, BRANCH), act(COL_K, BRANCH), act(COL_V, BRANCH), act(COL_LORA, LORA_W),
                  vec(BRANCH), vec(BRANCH), vec(BRANCH), vec(LORA_W),
                  vec(BRANCH), mat((64, BRANCH)), vec(BRANCH), mat((64, BRANCH)), mat((128, BRANCH)),
                  vec(BRANCH), vec(BRANCH), vec(BRANCH), vec(BRANCH), vec(BRANCH)],
        out_specs=pl.BlockSpec((1, tt, BRANCH), lambda b, t: (b, t, 0)),
        out_shape=jax.ShapeDtypeStruct((bsz, t_len, BRANCH), BF16),
        scratch_shapes=[pltpu.VMEM((8, 3 * BRANCH), F32), pltpu.VMEM((8, LORA_W), F32),
                        pltpu.VMEM((N_GROUPS, HEAD_DIM, GROUP_HEADS * HEAD_DIM), F32)]
                       + [pltpu.VMEM((tt, BRANCH), F32) for _ in range(7)],
        compiler_params=_cparams(("parallel", "arbitrary")),
        name="rwkv7",
    )(proj3, proj3, proj3, proj3, mu_r, mu_k, mu_v, mu_lo,
      row(w0), w2d.astype(BF16), row(a0), w2i.astype(BF16), w2g.astype(BF16),
      row(k_k), row(k_a), row(r_k), row(lnx_g), row(lnx_b))


def _forget_kernel(x_ref, wf_ref, bf_ref, nat_ref, carry_ref):
    tt = x_ref.shape[1]

    @pl.when(pl.program_id(1) == 0)
    def _():
        carry_ref[...] = jnp.zeros_like(carry_ref)

    f = jnp.dot(x_ref[0].astype(BF16), wf_ref[...], preferred_element_type=F32) + bf_ref[...]
    log_f = jnp.minimum(f, 0.0) - jnp.log1p(jnp.exp(-jnp.abs(f)))
    ri = lax.broadcasted_iota(jnp.int32, (tt, tt), 0)
    ci = lax.broadcasted_iota(jnp.int32, (tt, tt), 1)
    ltri = jnp.where(ri >= ci, 1.0, 0.0).astype(BF16)
    cum = carry_ref[0:1, :]
    for part in _split3(log_f):
        cum = cum + jnp.dot(ltri, part, preferred_element_type=F32)
    carry_ref[0:1, :] = cum[tt - 1:tt, :]
    nat_ref[0] = cum


def _forget_cumsum(x3, w_f, b_forget, tt=512):
    bsz, t_len, _ = x3.shape
    tt = min(tt, t_len)
    wf = jnp.zeros((D_MODEL, LANES), BF16).at[:, :N_HEADS].set(w_f.astype(BF16))
    bf = jnp.zeros((1, LANES), F32).at[0, :N_HEADS].set(b_forget)
    return pl.pallas_call(
        _forget_kernel,
        grid=(bsz, t_len // tt),
        in_specs=[pl.BlockSpec((1, tt, D_MODEL), lambda b, t: (b, t, 0)),
                  pl.BlockSpec((D_MODEL, LANES), lambda b, t: (0, 0)),
                  pl.BlockSpec((1, LANES), lambda b, t: (0, 0))],
        out_specs=pl.BlockSpec((1, tt, LANES), lambda b, t: (b, t, 0)),
        out_shape=jax.ShapeDtypeStruct((bsz, t_len, LANES), F32),
        scratch_shapes=[pltpu.VMEM((8, LANES), F32)],
        compiler_params=_cparams(("parallel", "arbitrary")),
        name="forget_cumsum",
    )(x3, wf, bf)


AUG_LANE = HEAD_DIM
VT_ROWS = HEAD_DIM + 16
KV_UNROLL = 2
SCORE_LOOKAHEAD = 2
LOG2E = math.log2(math.e)
UNDERFLOW_LOG2 = 152.0
NORM_SLACK = 1.01


def _bf16_pieces(x):
    p1 = x.astype(BF16).astype(F32)
    p2 = (x - p1).astype(BF16).astype(F32)
    p3 = (x - p1 - p2).astype(BF16).astype(F32)
    return p1, p2, p3


def _fox_prep_kernel(q_ref, k_ref, v_ref, cnat_ref, qa_ref, ka_ref, vt_ref, qn_ref, kn_ref):
    tt = q_ref.shape[1]
    q = q_ref[0].astype(F32) * (HEAD_DIM ** -0.5 * LOG2E)
    k = k_ref[0]
    v = v_ref[0]
    cnat = cnat_ref[0] * LOG2E
    lane = lax.broadcasted_iota(jnp.int32, (tt, LANES), 1)
    eye = jnp.where(lax.broadcasted_iota(jnp.int32, (HEAD_DIM, HEAD_DIM), 0)
                    == lax.broadcasted_iota(jnp.int32, (HEAD_DIM, HEAD_DIM), 1), 1.0, 0.0).astype(BF16)
    pad = jnp.zeros((tt, LANES - HEAD_DIM), F32)
    ones_rows = jnp.where(lax.broadcasted_iota(jnp.int32, (VT_ROWS - HEAD_DIM, tt), 0) == 0, 1.0, 0.0)

    head_cols = jnp.where(lax.broadcasted_iota(jnp.int32, (BRANCH, LANES), 0) // HEAD_DIM
                          == lax.broadcasted_iota(jnp.int32, (BRANCH, LANES), 1), 1.0, 0.0).astype(BF16)

    def max_sq_norms(x):
        return jnp.max(_dot(x * x, head_cols), axis=0, keepdims=True)

    q_rounded = q.astype(BF16).astype(F32)
    qn_ref[0, 0] = jnp.broadcast_to(max_sq_norms(q_rounded), (N_HEADS, LANES))
    kn_ref[0, 0] = jnp.broadcast_to(max_sq_norms(k.astype(F32)), (N_HEADS, LANES))
    for h in range(N_HEADS):
        sl = slice(h * HEAD_DIM, (h + 1) * HEAD_DIM)
        c1, c2, c3 = _bf16_pieces(cnat[:, h:h + 1])
        ones_q = (lane >= AUG_LANE + 3) & (lane < AUG_LANE + 6)
        aug_q = jnp.where(lane == AUG_LANE, c1, jnp.where(lane == AUG_LANE + 1, c2,
                          jnp.where(lane == AUG_LANE + 2, c3, jnp.where(ones_q, 1.0, 0.0))))
        q_h = q[:, sl].astype(BF16).astype(F32)
        k_h = k[:, sl].astype(F32)
        qa_ref[0, h] = jnp.where(lane < HEAD_DIM, jnp.concatenate([q_h, pad], axis=1), aug_q).astype(BF16)
        ones_k = (lane >= AUG_LANE) & (lane < AUG_LANE + 3)
        aug_k = jnp.where(lane == AUG_LANE + 3, -c1, jnp.where(lane == AUG_LANE + 4, -c2,
                          jnp.where(lane == AUG_LANE + 5, -c3, jnp.where(ones_k, 1.0, 0.0))))
        ka_ref[0, h] = jnp.where(lane < HEAD_DIM, jnp.concatenate([k_h, pad], axis=1), aug_k).astype(BF16)
        vt_ref[0, h, 0] = jnp.concatenate([_dot_nt(eye, v[:, sl]), ones_rows], axis=0).astype(BF16)


def _fox_prep(proj3, cum_nat, tt):
    bsz, t_len, _ = proj3.shape
    n_t = t_len // tt
    act = lambda col: pl.BlockSpec((1, tt, BRANCH), lambda b, t: (b, t, col // BRANCH))
    norm_spec = pl.BlockSpec((1, 1, N_HEADS, LANES), lambda b, t: (b, t, 0, 0))
    return pl.pallas_call(
        _fox_prep_kernel,
        grid=(bsz, n_t),
        in_specs=[act(COL_FQ), act(COL_FK), act(COL_FV),
                  pl.BlockSpec((1, tt, LANES), lambda b, t: (b, t, 0))],
        out_specs=[pl.BlockSpec((1, N_HEADS, tt, LANES), lambda b, t: (b, 0, t, 0)),
                   pl.BlockSpec((1, N_HEADS, tt, LANES), lambda b, t: (b, 0, t, 0)),
                   pl.BlockSpec((1, N_HEADS, 1, VT_ROWS, tt), lambda b, t: (b, 0, t, 0, 0)),
                   norm_spec, norm_spec],
        out_shape=[jax.ShapeDtypeStruct((bsz, N_HEADS, t_len, LANES), BF16),
                   jax.ShapeDtypeStruct((bsz, N_HEADS, t_len, LANES), BF16),
                   jax.ShapeDtypeStruct((bsz, N_HEADS, n_t, VT_ROWS, tt), BF16),
                   jax.ShapeDtypeStruct((bsz, n_t, N_HEADS, LANES), F32),
                   jax.ShapeDtypeStruct((bsz, n_t, N_HEADS, LANES), F32)],
        compiler_params=_cparams(("parallel", "parallel")),
        name="fox_prep",
    )(proj3, proj3, proj3, cum_nat)


def _fox_kernel(first_ref, qa_ref, ka_ref, vt_ref, o_ref, m_ref, acc_ref, *, tq, tk):
    b, hp, i = pl.program_id(0), pl.program_id(1), pl.program_id(2)
    n_chunks = tq // tk
    diag_mask = (lax.broadcasted_iota(jnp.int32, (tk, tk), 0)
                 <= lax.broadcasted_iota(jnp.int32, (tk, tk), 1))

    m_ref[...] = jnp.full(m_ref.shape, -1e30, F32)
    acc_ref[...] = jnp.zeros(acc_ref.shape, F32)

    def scores(j, hh, c):
        rows = pl.ds(pl.multiple_of(j * tk, tk), tk)
        return lax.dot_general(ka_ref[0, hh, rows, :], qa_ref[0, hh, c * tk:(c + 1) * tk, :],
                               (((1,), (1,)), ((), ())), preferred_element_type=F32)

    def update(j, hh, c, s, masked):
        if masked:
            s = jnp.where(diag_mask, s, -1e30)
        m = m_ref[hh, c, 0:1, :]
        m_new = jnp.maximum(m, jnp.max(s, axis=0, keepdims=True))
        alpha = jnp.exp2(m - m_new)
        p = jnp.exp2((s - m_new).astype(BF16))
        m_ref[hh, c, 0:1, :] = m_new
        acc_ref[hh, c] = alpha * acc_ref[hh, c] + jnp.dot(vt_ref[0, hh, j], p, preferred_element_type=F32)

    def step(j, hh, d):
        units = list(range(0 if d is None else d, n_chunks))
        pending = [scores(j, hh, c) for c in units[:SCORE_LOOKAHEAD]]
        for k, c in enumerate(units):
            if k + SCORE_LOOKAHEAD < len(units):
                pending.append(scores(j, hh, units[k + SCORE_LOOKAHEAD]))
            update(j, hh, c, pending.pop(0), c == d)

    n_full = i * n_chunks
    for hh in range(2):
        first = first_ref[(b * N_HEADS + hp * 2 + hh) * pl.num_programs(2) + i]

        def body(j2, carry, hh=hh):
            for u in range(KV_UNROLL):
                step(j2 * KV_UNROLL + u, hh, None)
            return carry

        lax.fori_loop(first // KV_UNROLL, n_full // KV_UNROLL, body, 0)
        for d in range(n_chunks):
            step(n_full + d, hh, d)
    out_t = jnp.concatenate(
        [jnp.concatenate([acc_ref[hh, c, 0:HEAD_DIM, :] / acc_ref[hh, c, HEAD_DIM:HEAD_DIM + 1, :]
                          for c in range(n_chunks)], axis=1) for hh in range(2)], axis=0)
    o_ref[0] = jnp.transpose(out_t).astype(o_ref.dtype)


def _first_key_blocks(cum_nat, q_norm2, k_norm2, tq, tk):
    q_max = jnp.max(q_norm2[:, :, 0, :N_HEADS], axis=1)
    k_max = jnp.max(k_norm2[:, :, 0, :N_HEADS], axis=1)
    bound = jnp.sqrt(q_max * k_max) * NORM_SLACK
    cum2 = cum_nat[:, :, :N_HEADS] * LOG2E
    cq_start = cum2[:, ::tq, :]
    ck_end = cum2[:, tk - 1::tk, :]
    bias_ub = cq_start[:, :, None, :] - ck_end[:, None, :, :]
    n_q, n_k = cq_start.shape[1], ck_end.shape[1]
    below_diag = jnp.arange(n_k)[None, :] < (jnp.arange(n_q) * (tq // tk))[:, None]
    dead = (bias_ub < -(2.0 * bound[:, None, None, :] + UNDERFLOW_LOG2)) & below_diag[None, :, :, None]
    first = jnp.sum(dead.astype(jnp.int32), axis=2)
    return jnp.transpose(first, (0, 2, 1)).reshape(-1)


def _fox_attention(proj3, cum_nat, tq=2048, tk=512):
    bsz, t_len, _ = proj3.shape
    tq = min(tq, t_len)
    tk = min(tk, tq)
    q_aug, k_aug, v_t, q_norm2, k_norm2 = _fox_prep(proj3, cum_nat, tk)
    first = _first_key_blocks(cum_nat, q_norm2, k_norm2, tq, tk)
    grid_spec = pltpu.PrefetchScalarGridSpec(
        num_scalar_prefetch=1,
        grid=(bsz, N_HEADS // 2, t_len // tq),
        in_specs=[pl.BlockSpec((1, 2, tq, LANES), lambda b, hp, i, first: (b, hp, i, 0)),
                  pl.BlockSpec((1, 2, t_len, LANES), lambda b, hp, i, first: (b, hp, 0, 0)),
                  pl.BlockSpec((1, 2, t_len // tk, VT_ROWS, tk), lambda b, hp, i, first: (b, hp, 0, 0, 0))],
        out_specs=pl.BlockSpec((1, tq, LANES), lambda b, hp, i, first: (b, i, hp)),
        scratch_shapes=[pltpu.VMEM((2, tq // tk, 8, tk), F32), pltpu.VMEM((2, tq // tk, VT_ROWS, tk), F32)])
    return pl.pallas_call(
        functools.partial(_fox_kernel, tq=tq, tk=tk),
        grid_spec=grid_spec,
        out_shape=jax.ShapeDtypeStruct((bsz, t_len, BRANCH), BF16),
        compiler_params=_cparams(("parallel", "parallel", "arbitrary")),
        name="fox_attention",
    )(first, q_aug, k_aug, v_t)


def _merge_kernel(oa_ref, cb_ref, cc_ref, ch_ref, ccp_ref, chp_ref, oc_ref,
                  g0_ref, g1_ref, g2_ref, x_ref, convw_ref, bg_ref,
                  wua_ref, wub_ref, wuc_ref, wo_ref, lg_ref, lb_ref, o_ref, *, alpha, tiles_per_seq):
    tm = x_ref.shape[0]
    first = (pl.program_id(0) % tiles_per_seq) == 0
    u = cc_ref[...].astype(F32) * ch_ref[...].astype(F32)
    u_prev = ccp_ref[...].astype(F32) * chp_ref[...].astype(F32)
    u_prev = jnp.where(first, 0.0, u_prev)
    row = lax.broadcasted_iota(jnp.int32, u.shape, 0)
    u1 = jnp.where(row == 0, u_prev[7:8, :], pltpu.roll(u, 1, 0))
    u2 = pltpu.roll(u, 2, 0)
    u2 = jnp.where(row == 0, u_prev[6:7, :], jnp.where(row == 1, u_prev[7:8, :], u2))
    cw = convw_ref[...]
    conv = cw[0:1, :] * u2 + cw[1:2, :] * u1 + cw[2:3, :] * u
    o_b = cb_ref[...].astype(F32) * conv

    bg = bg_ref[...]
    merged = _sigmoid(g0_ref[...].astype(F32) + bg[0:1, :]) * jnp.dot(
        oa_ref[...], wua_ref[...], preferred_element_type=F32)
    merged = merged + _sigmoid(g1_ref[...].astype(F32) + bg[1:2, :]) * _dot(o_b, wub_ref[...])
    merged = merged + _sigmoid(g2_ref[...].astype(F32) + bg[2:3, :]) * jnp.dot(
        oc_ref[...], wuc_ref[...], preferred_element_type=F32)
    mix = _dot(merged, wo_ref[...])
    o_ref[...] = _layer_norm(alpha * x_ref[...] + mix, lg_ref[...], lb_ref[...])


def _merge(o_a, proj, o_c, x2d, conv_w, b_gate, wua, wub, wuc, wo, ln_g, ln_b, alpha, t_len, tm=512):
    n = x2d.shape[0]
    tm = min(tm, t_len)
    tiles_per_seq = t_len // tm
    act = lambda col, width: pl.BlockSpec((tm, width), lambda i: (i, col // width))
    halo = lambda col: pl.BlockSpec(
        (8, BRANCH), lambda i: (jnp.maximum(i * (tm // 8) - 1, 0), col // BRANCH))
    full = lambda shape: pl.BlockSpec(shape, lambda i: (0, 0))
    row = lambda a: a.reshape(1, -1).astype(F32)
    return pl.pallas_call(
        functools.partial(_merge_kernel, alpha=alpha, tiles_per_seq=tiles_per_seq),
        grid=(n // tm,),
        in_specs=[act(0, BRANCH), act(COL_CB, BRANCH), act(COL_CC, BRANCH), act(COL_CH, BRANCH),
                  halo(COL_CC), halo(COL_CH), act(0, BRANCH),
                  act(COL_GATE, D_MODEL), act(COL_GATE + D_MODEL, D_MODEL),
                  act(COL_GATE + 2 * D_MODEL, D_MODEL),
                  act(0, D_MODEL), full((3, BRANCH)), full((3, D_MODEL)),
                  full((BRANCH, D_MODEL)), full((BRANCH, D_MODEL)), full((BRANCH, D_MODEL)),
                  full((D_MODEL, D_MODEL)), full((1, D_MODEL)), full((1, D_MODEL))],
        out_specs=pl.BlockSpec((tm, D_MODEL), lambda i: (i, 0)),
        out_shape=jax.ShapeDtypeStruct((n, D_MODEL), F32),
        compiler_params=_cparams(("parallel",)),
        name="merge",
    )(o_a, proj, proj, proj, proj, proj, o_c, proj, proj, proj, x2d,
      conv_w.astype(F32), b_gate.astype(F32),
      wua.astype(BF16), wub.astype(BF16), wuc.astype(BF16), wo.astype(BF16), row(ln_g), row(ln_b))


def _ffn_kernel(x_ref, w1_ref, w3_ref, w2_ref, lg_ref, lb_ref, o_ref, xb_ref, acc_ref, *, alpha):
    j = pl.program_id(1)

    @pl.when(j == 0)
    def _():
        xb_ref[...] = x_ref[...].astype(BF16)
        acc_ref[...] = jnp.zeros_like(acc_ref)

    xb = xb_ref[...]
    h1 = jnp.dot(xb, w1_ref[...], preferred_element_type=F32)
    h3 = jnp.dot(xb, w3_ref[...], preferred_element_type=F32)
    hidden = (h1 * _sigmoid(h1) * h3).astype(BF16)
    acc_ref[...] += jnp.dot(hidden, w2_ref[...], preferred_element_type=F32)

    @pl.when(j == pl.num_programs(1) - 1)
    def _():
        o_ref[...] = _layer_norm(alpha * x_ref[...] + acc_ref[...], lg_ref[...], lb_ref[...])


def _ffn_dense(x2d, w1, w3, w2, ln_g, ln_b, alpha, tm=1024, tf=256):
    n = x2d.shape[0]
    tm = min(tm, n)
    d_ff = w1.shape[1]
    row = lambda a: a.reshape(1, -1).astype(F32)
    return pl.pallas_call(
        functools.partial(_ffn_kernel, alpha=alpha),
        grid=(n // tm, d_ff // tf),
        in_specs=[pl.BlockSpec((tm, D_MODEL), lambda i, j: (i, 0)),
                  pl.BlockSpec((D_MODEL, tf), lambda i, j: (0, j)),
                  pl.BlockSpec((D_MODEL, tf), lambda i, j: (0, j)),
                  pl.BlockSpec((tf, D_MODEL), lambda i, j: (j, 0)),
                  pl.BlockSpec((1, D_MODEL), lambda i, j: (0, 0)),
                  pl.BlockSpec((1, D_MODEL), lambda i, j: (0, 0))],
        out_specs=pl.BlockSpec((tm, D_MODEL), lambda i, j: (i, 0)),
        out_shape=jax.ShapeDtypeStruct((n, D_MODEL), F32),
        scratch_shapes=[pltpu.VMEM((tm, D_MODEL), BF16), pltpu.VMEM((tm, D_MODEL), F32)],
        compiler_params=_cparams(("parallel", "arbitrary")),
        name="ffn_dense",
    )(x2d, w1.astype(BF16), w3.astype(BF16), w2.astype(BF16), row(ln_g), row(ln_b))


def _router_kernel(x_ref, w_ref, b_ref, meta_ref, gate_ref, total_ref, count_ref):
    @pl.when(pl.program_id(0) == 0)
    def _():
        count_ref[...] = jnp.zeros_like(count_ref)

    x = x_ref[...]
    tm = x.shape[0]
    logits = b_ref[...] + jnp.zeros((tm, LANES), F32)
    w_parts = (w_ref[0], w_ref[1], w_ref[2])
    x_parts = _split3(x)
    for xi in range(2):
        for wi in range(2 - xi):
            logits = logits + jnp.dot(x_parts[xi], w_parts[wi], preferred_element_type=F32)
    lane = lax.broadcasted_iota(jnp.int32, (tm, LANES), 1)
    neg = -1e30
    logits = jnp.where(lane < N_EXPERTS, logits, neg)
    m1 = jnp.max(logits, axis=1, keepdims=True)
    i1 = jnp.min(jnp.where(logits == m1, lane, LANES), axis=1, keepdims=True)
    rest = jnp.where(lane == i1, neg, logits)
    m2 = jnp.max(rest, axis=1, keepdims=True)
    i2 = jnp.min(jnp.where(rest == m2, lane, LANES), axis=1, keepdims=True)
    e2 = jnp.exp(m2 - m1)
    g1 = 1.0 / (1.0 + e2)
    g2 = e2 / (1.0 + e2)
    chosen = jnp.where((lane == i1) | (lane == i2), 1.0, 0.0)
    lstrict = jnp.where(lax.broadcasted_iota(jnp.int32, (tm, tm), 0)
                        > lax.broadcasted_iota(jnp.int32, (tm, tm), 1), 1.0, 0.0).astype(BF16)
    before = count_ref[0:1, :] + jnp.dot(lstrict, chosen.astype(BF16), preferred_element_type=F32)
    count_ref[0:1, :] = before[tm - 1:tm, :] + chosen[tm - 1:tm, :]
    total_ref[...] = jnp.broadcast_to(count_ref[0:1, :], total_ref.shape)
    rank1 = jnp.sum(jnp.where(lane == i1, before, 0.0), axis=1, keepdims=True)
    rank2 = jnp.sum(jnp.where(lane == i2, before, 0.0), axis=1, keepdims=True)
    meta = jnp.where(lane == 0, i1.astype(F32), jnp.where(lane == 1, i2.astype(F32),
                     jnp.where(lane == 2, rank1, jnp.where(lane == 3, rank2, 0.0))))
    meta_ref[...] = meta.astype(jnp.int32)
    gate_ref[...] = jnp.where(lane == 0, g1, jnp.where(lane == 1, g2, 0.0))


def _router(x2d, router_w, router_b, tm=512):
    n = x2d.shape[0]
    tm = min(tm, n)
    w_pad = jnp.zeros((D_MODEL, LANES), F32).at[:, :N_EXPERTS].set(router_w)
    w_parts = jnp.stack(_split3(w_pad))
    b_pad = jnp.zeros((1, LANES), F32).at[0, :N_EXPERTS].set(router_b)
    return pl.pallas_call(
        _router_kernel,
        grid=(n // tm,),
        in_specs=[pl.BlockSpec((tm, D_MODEL), lambda i: (i, 0)),
                  pl.BlockSpec((3, D_MODEL, LANES), lambda i: (0, 0, 0)),
                  pl.BlockSpec((1, LANES), lambda i: (0, 0))],
        out_specs=[pl.BlockSpec((tm, LANES), lambda i: (i, 0)),
                   pl.BlockSpec((tm, LANES), lambda i: (i, 0)),
                   pl.BlockSpec((8, LANES), lambda i: (0, 0))],
        out_shape=[jax.ShapeDtypeStruct((n, LANES), jnp.int32),
                   jax.ShapeDtypeStruct((n, LANES), F32),
                   jax.ShapeDtypeStruct((8, LANES), F32)],
        scratch_shapes=[pltpu.VMEM((8, LANES), F32)],
        compiler_params=_cparams(("arbitrary",)),
        name="router",
    )(x2d, w_parts, b_pad)


DMA_UNROLL = 8


def _row_copies(src_ref, dst_ref, sem, src_row, dst_row):
    return pltpu.make_async_copy(src_ref.at[pl.ds(src_row, 1)], dst_ref.at[pl.ds(dst_row, 1)], sem)


def _dispatch_kernel(pos0_ref, pos1_ref, x_ref, zero_ref, xs_ref, sem):
    del zero_ref
    tm = x_ref.shape[0]

    def copies(r):
        return (_row_copies(x_ref, xs_ref, sem, r, pos0_ref[r]),
                _row_copies(x_ref, xs_ref, sem, r, pos1_ref[r]))

    def start(r, carry):
        for cp in copies(r):
            cp.start()
        return carry

    def wait(r, carry):
        for cp in copies(r):
            cp.wait()
        return carry

    lax.fori_loop(0, tm, start, 0, unroll=DMA_UNROLL)
    lax.fori_loop(0, tm, wait, 0, unroll=DMA_UNROLL)


def _dispatch(x2d, pos0, pos1, n_sorted, tm=512):
    n = x2d.shape[0]
    tm = min(tm, n)
    smem = lambda: pl.BlockSpec((tm,), lambda i: (i,), memory_space=pltpu.SMEM)
    return pl.pallas_call(
        _dispatch_kernel,
        grid=(n // tm,),
        in_specs=[smem(), smem(), pl.BlockSpec((tm, D_MODEL), lambda i: (i, 0)),
                  pl.BlockSpec(memory_space=pl.ANY)],
        out_specs=pl.BlockSpec(memory_space=pl.ANY),
        out_shape=jax.ShapeDtypeStruct((n_sorted, D_MODEL), F32),
        scratch_shapes=[pltpu.SemaphoreType.DMA(())],
        input_output_aliases={3: 0},
        compiler_params=_cparams(("arbitrary",)),
        name="moe_dispatch",
    )(pos0, pos1, x2d, jnp.zeros((n_sorted, D_MODEL), F32))


def _experts_kernel(te_ref, used_ref, x_ref, w1_ref, w3_ref, w2_ref, o_ref, xb_ref, acc_ref):
    i = pl.program_id(0)
    j = pl.program_id(1)
    last = pl.num_programs(1) - 1
    live = i < used_ref[0]

    @pl.when(live & (j == 0))
    def _():
        xb_ref[...] = x_ref[...].astype(BF16)
        acc_ref[...] = jnp.zeros_like(acc_ref)

    @pl.when(live)
    def _():
        xb = xb_ref[...]
        h1 = jnp.dot(xb, w1_ref[0].astype(BF16), preferred_element_type=F32)
        h3 = jnp.dot(xb, w3_ref[0].astype(BF16), preferred_element_type=F32)
        hidden = (h1 * _sigmoid(h1) * h3).astype(BF16)
        acc_ref[...] += jnp.dot(hidden, w2_ref[0].astype(BF16), preferred_element_type=F32)

    @pl.when(live & (j == last))
    def _():
        o_ref[...] = acc_ref[...]

    @pl.when(jnp.logical_not(live) & (j == last))
    def _():
        o_ref[...] = jnp.zeros_like(o_ref)


def _experts(xs, tile_expert, n_used, w1, w3, w2, tm, tf=512):
    n_sorted = xs.shape[0]
    d_ff = w1.shape[2]
    n_f = d_ff // tf
    def jj(i, j, used):
        return jnp.where(i < used[0], j, n_f - 1)
    grid_spec = pltpu.PrefetchScalarGridSpec(
        num_scalar_prefetch=2,
        grid=(n_sorted // tm, n_f),
        in_specs=[pl.BlockSpec((tm, D_MODEL), lambda i, j, te, used: (jnp.minimum(i, used[0] - 1), 0)),
                  pl.BlockSpec((1, D_MODEL, tf), lambda i, j, te, used: (te[i], 0, jj(i, j, used))),
                  pl.BlockSpec((1, D_MODEL, tf), lambda i, j, te, used: (te[i], 0, jj(i, j, used))),
                  pl.BlockSpec((1, tf, D_MODEL), lambda i, j, te, used: (te[i], jj(i, j, used), 0))],
        out_specs=pl.BlockSpec((tm, D_MODEL), lambda i, j, te, used: (i, 0)),
        scratch_shapes=[pltpu.VMEM((tm, D_MODEL), BF16), pltpu.VMEM((tm, D_MODEL), F32)])
    return pl.pallas_call(
        _experts_kernel,
        grid_spec=grid_spec,
        out_shape=jax.ShapeDtypeStruct((n_sorted, D_MODEL), F32),
        compiler_params=_cparams(("arbitrary", "arbitrary")),
        name="moe_experts",
    )(tile_expert, n_used, xs, w1, w3, w2)


def _combine_kernel(pos0_ref, pos1_ref, x_ref, gate_ref, lg_ref, lb_ref, ys_ref, o_ref, y0_ref, y1_ref, sem,
                    *, alpha):
    tm = x_ref.shape[0]

    def copies(r):
        return (_row_copies(ys_ref, y0_ref, sem, pos0_ref[r], r),
                _row_copies(ys_ref, y1_ref, sem, pos1_ref[r], r))

    def start(r, carry):
        for cp in copies(r):
            cp.start()
        return carry

    def wait(r, carry):
        for cp in copies(r):
            cp.wait()
        return carry

    lax.fori_loop(0, tm, start, 0, unroll=DMA_UNROLL)
    lax.fori_loop(0, tm, wait, 0, unroll=DMA_UNROLL)
    gates = gate_ref[...]
    ffn = gates[:, 0:1] * y0_ref[...] + gates[:, 1:2] * y1_ref[...]
    o_ref[...] = _layer_norm(alpha * x_ref[...] + ffn, lg_ref[...], lb_ref[...])


def _combine(x2d, ys, pos0, pos1, gates, ln_g, ln_b, alpha, tm=256):
    n = x2d.shape[0]
    tm = min(tm, n)
    smem = lambda: pl.BlockSpec((tm,), lambda i: (i,), memory_space=pltpu.SMEM)
    row = lambda a: a.reshape(1, -1).astype(F32)
    return pl.pallas_call(
        functools.partial(_combine_kernel, alpha=alpha),
        grid=(n // tm,),
        in_specs=[smem(), smem(), pl.BlockSpec((tm, D_MODEL), lambda i: (i, 0)),
                  pl.BlockSpec((tm, LANES), lambda i: (i, 0)),
                  pl.BlockSpec((1, D_MODEL), lambda i: (0, 0)), pl.BlockSpec((1, D_MODEL), lambda i: (0, 0)),
                  pl.BlockSpec(memory_space=pl.ANY)],
        out_specs=pl.BlockSpec((tm, D_MODEL), lambda i: (i, 0)),
        out_shape=jax.ShapeDtypeStruct((n, D_MODEL), F32),
        scratch_shapes=[pltpu.VMEM((tm, D_MODEL), F32), pltpu.VMEM((tm, D_MODEL), F32),
                        pltpu.SemaphoreType.DMA(())],
        compiler_params=_cparams(("arbitrary",)),
        name="moe_combine",
    )(pos0, pos1, x2d, gates, row(ln_g), row(ln_b), ys)


def _ffn_moe(x2d, router_w, router_b, w1, w3, w2, ln_g, ln_b, alpha, tm=1024):
    n = x2d.shape[0]
    tm = min(tm, n)
    meta, gates, totals = _router(x2d, router_w, router_b)
    counts = totals[0, :N_EXPERTS].astype(jnp.int32)
    padded = ((counts + tm - 1) // tm) * tm
    ends = jnp.cumsum(padded)
    starts = ends - padded
    pos0 = starts[meta[:, 0]] + meta[:, 2]
    pos1 = starts[meta[:, 1]] + meta[:, 3]
    n_sorted = 2 * n + N_EXPERTS * tm
    tile_start = jnp.arange(n_sorted // tm, dtype=jnp.int32) * tm
    tile_expert = jnp.minimum(jnp.sum((ends[None, :] <= tile_start[:, None]).astype(jnp.int32), axis=1),
                              N_EXPERTS - 1)
    n_used = (ends[-1:] // tm).astype(jnp.int32)
    xs = _dispatch(x2d, pos0, pos1, n_sorted)
    ys = _experts(xs, tile_expert, n_used, w1, w3, w2, tm)
    return _combine(x2d, ys, pos0, pos1, gates, ln_g, ln_b, alpha)


def _arrange_w_in(w_in_l):
    rwkv_cols = 3 * BRANCH + LORA_W
    conv0 = rwkv_cols
    fox0 = conv0 + 3 * BRANCH
    f0 = fox0 + 3 * BRANCH
    gate0 = f0 + N_HEADS
    zeros = jnp.zeros((D_MODEL, COL_CB - COL_LORA - LORA_W), w_in_l.dtype)
    w_all = jnp.concatenate([w_in_l[:, :rwkv_cols], zeros, w_in_l[:, conv0:f0], w_in_l[:, gate0:]], axis=1)
    return w_all.astype(BF16), w_in_l[:, f0:gate0]


def kernel(x, ln0_g, ln0_b, w_in, mu_shift, w0_decay, w2_decay, a0, w2_iclr, w2_gate, k_k, k_a, r_k, lnx_g, lnx_b, conv_w, b_forget, b_gate, w_up_rwkv, w_up_conv, w_up_attn, w_out, ln1_g, ln1_b, ln2_g, ln2_b, ffn_w1, ffn_w3, ffn_w2, router_w, router_b, moe_w1, moe_w3, moe_w2):
    bsz, t_len, _ = x.shape
    n = bsz * t_len
    depth = w_in.shape[0]
    alpha = (2 * depth) ** 0.25
    xs = _ln0(x.reshape(n, D_MODEL), ln0_g, ln0_b)
    for l in range(depth):
        w_all, w_f = _arrange_w_in(w_in[l])
        proj = _proj(xs, w_all)
        proj3 = proj.reshape(bsz, t_len, PROJ_COLS)
        o_a = _rwkv(proj3, mu_shift[l], w0_decay[l], w2_decay[l], a0[l], w2_iclr[l], w2_gate[l],
                    k_k[l], k_a[l], r_k[l].reshape(-1), lnx_g[l], lnx_b[l])
        cum_nat = _forget_cumsum(xs.reshape(bsz, t_len, D_MODEL), w_f, b_forget[l])
        o_c = _fox_attention(proj3, cum_nat)
        xs = _merge(o_a.reshape(n, BRANCH), proj, o_c.reshape(n, BRANCH), xs, conv_w[l], b_gate[l],
                    w_up_rwkv[l], w_up_conv[l], w_up_attn[l], w_out[l], ln1_g[l], ln1_b[l], alpha, t_len)
        i = l // 2
        if l % 2 == 0:
            xs = _ffn_dense(xs, ffn_w1[i], ffn_w3[i], ffn_w2[i], ln2_g[l], ln2_b[l], alpha)
        else:
            xs = _ffn_moe(xs, router_w[i], router_b[i], moe_w1[i], moe_w3[i], moe_w2[i],
                          ln2_g[l], ln2_b[l], alpha)
    return xs.reshape(bsz, t_len, D_MODEL)
```
